```python
import jax, jax.numpy as jnp
from jax import lax
import numpy as np

D_MODEL = 1024
BATCH = 2
SEQ = 8192
DEPTH = 1
DEC_BATCH = 16
DEC_SEQ = 64
PAST_LEN = 4096

CHUNK = 64
H_RET = 8
DK_RET = 64
DV_RET = 64
H_DSA = 8
DH_DSA = 64
H_IDX = 8
D_IDX = 64
TOPK_MAX = 256
D_FF = 2816
CONV_W = 3
Q_BLOCK = 128
ROPE_BASE = 10000.0
EPS = 1e-6
W_RET = H_RET * DV_RET
W_DSA = H_DSA * DH_DSA
MIX_WIDTH = W_RET + W_DSA
P_IN = 2 * H_RET * DK_RET + 2 * W_RET + 3 * W_DSA + H_IDX * D_IDX + D_IDX + H_IDX

kernel_name = 'hymba_retention_dsa_streaming_step'


def _split_points():
    sizes = [H_RET * DK_RET, H_RET * DK_RET, W_RET, W_RET, W_DSA, W_DSA, W_DSA, H_IDX * D_IDX, D_IDX, H_IDX]
    return [int(s) for s in np.cumsum(sizes)[:-1]]


def rmsnorm(x, g):
    xf = x.astype(jnp.float32)
    y = xf * lax.rsqrt(jnp.mean(xf * xf, axis=-1, keepdims=True) + EPS)
    return (y * g.astype(jnp.float32)).astype(x.dtype)


def rotary(x, pos):
    half = x.shape[-1] // 2
    inv_freq = ROPE_BASE ** (-jnp.arange(half, dtype=jnp.float32) / half)
    ang = pos.astype(jnp.float32)[:, None] * inv_freq[None, :]
    cos = jnp.cos(ang)[None, :, None, :]
    sin = jnp.sin(ang)[None, :, None, :]
    xf = x.astype(jnp.float32)
    x1, x2 = xf[..., :half], xf[..., half:]
    return jnp.concatenate([x1 * cos - x2 * sin, x1 * sin + x2 * cos], axis=-1)


def retention_log_decay():
    return jnp.log1p(-jnp.exp2(-5.0 - jnp.arange(H_RET, dtype=jnp.float32)))


def retention_chunk(q, k, v, s0):
    c = q.shape[1]
    lg = retention_log_decay()
    idx = jnp.arange(c, dtype=jnp.float32)
    diff = idx[:, None] - idx[None, :]
    decay = jnp.where(diff[None] >= 0, jnp.exp(lg[:, None, None] * jnp.maximum(diff, 0.0)[None]), 0.0)
    scores = jnp.einsum('bthd,bshd->bhts', q, k) * decay[None]
    inner = jnp.einsum('bhts,bshv->bthv', scores, v)
    xi = jnp.exp(lg[None, :] * (idx + 1.0)[:, None])
    cross = jnp.einsum('bthd,bhdv->bthv', q, s0) * xi[None, :, :, None]
    zeta = jnp.exp(lg[:, None] * (c - 1.0 - idx)[None, :])
    s_new = jnp.exp(lg * c)[None, :, None, None] * s0 + jnp.einsum('bshd,hs,bshv->bhdv', k, zeta, v)
    return inner + cross, s_new


def retention(q, k, v, s0):
    B, T = q.shape[:2]
    c = min(CHUNK, T)
    nc = T // c

    def to_chunks(a):
        return jnp.moveaxis(a.reshape(B, nc, c, *a.shape[2:]), 1, 0)

    def step(s, blk):
        qc, kc, vc = blk
        o, s = retention_chunk(qc, kc, vc, s)
        return s, o

    s_fin, o = lax.scan(step, s0, (to_chunks(q), to_chunks(k), to_chunks(v)))
    o = jnp.moveaxis(o, 0, 1).reshape(B, T, H_RET, DV_RET)
    return o, s_fin


def retention_output(o, gate, gn_gain):
    B, T = o.shape[:2]
    mu = jnp.mean(o, axis=-1, keepdims=True)
    var = jnp.mean(jnp.square(o - mu), axis=-1, keepdims=True)
    on = ((o - mu) * lax.rsqrt(var + EPS)).reshape(B, T, W_RET) * gn_gain.astype(jnp.float32)
    return jax.nn.silu(gate.astype(jnp.float32)) * on


def dsa_block(q, qi, wi, q_chunk, keys, vals, keys_idx, k_chunk, topk):
    dots = jnp.einsum('bthd,bsd->bths', qi.astype(jnp.float32), keys_idx.astype(jnp.float32)) * (D_IDX ** -0.5)
    score = jnp.einsum('bths,bth->bts', jax.nn.relu(dots), wi.astype(jnp.float32) * (H_IDX ** -0.5))
    admissible = k_chunk[None, :] <= q_chunk[:, None]
    score = jnp.where(admissible[None], score, -jnp.inf)
    _, sel = lax.top_k(score, topk)
    sel_ok = k_chunk[sel] <= q_chunk[None, :, None]
    gather = jax.vmap(lambda a, i: a[i])
    kg = gather(keys, sel)
    vg = gather(vals, sel)
    logits = jnp.einsum('bthd,btjhd->bthj', q.astype(jnp.float32), kg.astype(jnp.float32)) * (DH_DSA ** -0.5)
    logits = jnp.where(sel_ok[:, :, None, :], logits, -1e30)
    p = jax.nn.softmax(logits, axis=-1)
    return jnp.einsum('bthj,btjhd->bthd', p, vg.astype(jnp.float32))


def conv_ffn(h, conv_state, w_up, w_gate, w_down, conv_w, conv_b):
    T = h.shape[1]
    a = h @ w_up
    u = h @ w_gate
    ap = jnp.concatenate([conv_state.astype(a.dtype), a], axis=1)
    c = conv_b
    for j in range(CONV_W):
        c = c + ap[:, j:j + T] * conv_w[j]
    out = (jax.nn.silu(c) * u) @ w_down
    return out, ap[:, T:]


def trunk_layer(x, ret_state, past_k, past_v, past_ki, conv_state,
                w_in, w_out, g_mix, g_gn, g_ffn, w_up, w_gate, w_down, conv_w, conv_b):
    B, T, _ = x.shape
    P = past_k.shape[1]
    pos = P + jnp.arange(T, dtype=jnp.int32)
    h = rmsnorm(x, g_mix)
    proj = h @ w_in
    rq, rk, rv, rg, dq, dk, dv, iq, ik, iw = jnp.split(proj, _split_points(), axis=-1)
    rq = rotary(rq.reshape(B, T, H_RET, DK_RET), pos)
    rk = rotary(rk.reshape(B, T, H_RET, DK_RET), pos) * (DK_RET ** -0.5)
    rv = rv.reshape(B, T, H_RET, DV_RET).astype(jnp.float32)
    o_ret, ret_new = retention(rq, rk, rv, ret_state.astype(jnp.float32))
    y_ret = retention_output(o_ret, rg, g_gn)
    dq = dq.reshape(B, T, H_DSA, DH_DSA)
    dk = dk.reshape(B, T, H_DSA, DH_DSA)
    dv = dv.reshape(B, T, H_DSA, DH_DSA)
    iq = iq.reshape(B, T, H_IDX, D_IDX)
    keys = jnp.concatenate([past_k.astype(dk.dtype), dk], axis=1)
    vals = jnp.concatenate([past_v.astype(dv.dtype), dv], axis=1)
    keys_idx = jnp.concatenate([past_ki.astype(ik.dtype), ik], axis=1)
    L = P + T
    k_chunk = jnp.arange(L, dtype=jnp.int32) // CHUNK
    q_chunk = pos // CHUNK
    topk = min(TOPK_MAX, L // 4)
    qb = min(Q_BLOCK, T)
    nb = T // qb

    def to_blocks(a):
        return jnp.moveaxis(a.reshape(B, nb, qb, *a.shape[2:]), 1, 0)

    def run_block(blk):
        q_b, qi_b, wi_b, qc_b = blk
        return dsa_block(q_b, qi_b, wi_b, qc_b, keys, vals, keys_idx, k_chunk, topk)

    o_dsa = lax.map(run_block, (to_blocks(dq), to_blocks(iq), to_blocks(iw), q_chunk.reshape(nb, qb)))
    o_dsa = jnp.moveaxis(o_dsa, 0, 1).reshape(B, T, W_DSA)
    mix = jnp.concatenate([y_ret, o_dsa], axis=-1).astype(x.dtype)
    x = x + mix @ w_out
    f, conv_new = conv_ffn(rmsnorm(x, g_ffn), conv_state, w_up, w_gate, w_down, conv_w, conv_b)
    x = x + f
    return x, dk, dv, ik, ret_new, conv_new


def setup_inputs(seed: int = 0) -> dict:
    key = jax.random.key(seed)
    ks = jax.random.split(key, 18)
    f32 = jnp.float32

    def nrm(k, shape, scale):
        return jax.random.normal(k, shape, f32) * scale

    def gain(k, shape):
        return 1.0 + 0.01 * jax.random.normal(k, shape, f32)

    return {
        'x_prompt': nrm(ks[0], (BATCH, SEQ, D_MODEL), 1.0),
        'x_sample': nrm(ks[1], (DEC_BATCH, DEC_SEQ, D_MODEL), 1.0),
        'cache_dsa_k': nrm(ks[2], (DEPTH, DEC_BATCH, PAST_LEN, H_DSA, DH_DSA), 1.0),
        'cache_dsa_v': nrm(ks[3], (DEPTH, DEC_BATCH, PAST_LEN, H_DSA, DH_DSA), 1.0),
        'cache_idx_k': nrm(ks[4], (DEPTH, DEC_BATCH, PAST_LEN, D_IDX), 1.0),
        'state_ret': nrm(ks[5], (DEPTH, DEC_BATCH, H_RET, DK_RET, DV_RET), 1.0),
        'state_ffn_conv': nrm(ks[6], (DEPTH, DEC_BATCH, CONV_W - 1, D_FF), 1.0),
        'w_in': nrm(ks[7], (DEPTH, D_MODEL, P_IN), D_MODEL ** -0.5),
        'w_out': nrm(ks[8], (DEPTH, MIX_WIDTH, D_MODEL), MIX_WIDTH ** -0.5),
        'g_norm_mix': gain(ks[9], (DEPTH, D_MODEL)),
        'g_gn_ret': gain(ks[10], (DEPTH, W_RET)),
        'g_norm_ffn': gain(ks[11], (DEPTH, D_MODEL)),
        'w_up': nrm(ks[12], (DEPTH, D_MODEL, D_FF), D_MODEL ** -0.5),
        'w_gate': nrm(ks[13], (DEPTH, D_MODEL, D_FF), D_MODEL ** -0.5),
        'w_down': nrm(ks[14], (DEPTH, D_FF, D_MODEL), D_FF ** -0.5),
        'conv_w': nrm(ks[15], (DEPTH, CONV_W, D_FF), CONV_W ** -0.5),
        'conv_b': nrm(ks[16], (DEPTH, D_FF), 0.01),
        'g_norm_final': gain(ks[17], (D_MODEL,)),
    }


def reference(x_prompt, x_sample, cache_dsa_k, cache_dsa_v, cache_idx_k, state_ret, state_ffn_conv,
              w_in, w_out, g_norm_mix, g_gn_ret, g_norm_ffn, w_up, w_gate, w_down, conv_w, conv_b,
              g_norm_final):
    xp, xs = x_prompt, x_sample
    bp = xp.shape[0]
    dt = xp.dtype
    p_k, p_v, p_ki, p_ret, p_conv = [], [], [], [], []
    s_k, s_v, s_ki, s_ret, s_conv = [], [], [], [], []
    for l in range(DEPTH):
        weights = (w_in[l], w_out[l], g_norm_mix[l], g_gn_ret[l], g_norm_ffn[l],
                   w_up[l], w_gate[l], w_down[l], conv_w[l], conv_b[l])
        xp, k_new, v_new, ki_new, ret_new, conv_new = trunk_layer(
            xp, jnp.zeros((bp, H_RET, DK_RET, DV_RET), jnp.float32),
            jnp.zeros((bp, 0, H_DSA, DH_DSA), dt), jnp.zeros((bp, 0, H_DSA, DH_DSA), dt),
            jnp.zeros((bp, 0, D_IDX), dt), jnp.zeros((bp, CONV_W - 1, D_FF), dt), *weights)
        p_k.append(k_new); p_v.append(v_new); p_ki.append(ki_new); p_ret.append(ret_new); p_conv.append(conv_new)
        xs, k_new, v_new, ki_new, ret_new, conv_new = trunk_layer(
            xs, state_ret[l], cache_dsa_k[l], cache_dsa_v[l], cache_idx_k[l], state_ffn_conv[l], *weights)
        s_k.append(k_new); s_v.append(v_new); s_ki.append(ki_new); s_ret.append(ret_new); s_conv.append(conv_new)
    y_prompt = rmsnorm(xp, g_norm_final)
    y_sample = rmsnorm(xs, g_norm_final)
    return (y_prompt, y_sample,
            jnp.stack(p_k), jnp.stack(p_v), jnp.stack(p_ki), jnp.stack(p_ret), jnp.stack(p_conv),
            jnp.stack(s_k), jnp.stack(s_v), jnp.stack(s_ki), jnp.stack(s_ret), jnp.stack(s_conv))
```

```python
import functools

import numpy as np
import jax
import jax.numpy as jnp
from jax import lax
from jax.experimental import pallas as pl
from jax.experimental.pallas import tpu as pltpu

F32 = jnp.float32
BF16 = jnp.bfloat16
I32 = jnp.int32

CHUNK = 64
H_RET = 8
DK_RET = 64
DV_RET = 64
H_DSA = 8
DH_DSA = 64
H_IDX = 8
D_IDX = 64
TOPK_MAX = 256
CONV_W = 3
ROPE_BASE = 10000.0
EPS = 1e-6
W_RET = H_RET * DV_RET
W_DSA = H_DSA * DH_DSA

LANES = 128
VMEM_LIMIT_BYTES = 56 * 1024 * 1024
INT_MIN = -(2 ** 31)
NEG_BIG = -1e30


def _dot(a, b):
    return jnp.dot(a, b, preferred_element_type=F32)


def _dot_nt(a, b):
    return lax.dot_general(a, b, (((1,), (1,)), ((), ())), preferred_element_type=F32)


def _dot_tn(a, b):
    return lax.dot_general(a, b, (((0,), (0,)), ((), ())), preferred_element_type=F32)


def _rms(x):
    return x * lax.rsqrt(jnp.mean(x * x, axis=-1, keepdims=True) + EPS)


def _silu(x):
    return x / (1.0 + jnp.exp(-x))


def _inproj_kernel(x_ref, g_ref, w_ref, cos_ref, sin_ref,
                   rq_ref, rk_ref, rv_ref, rg_ref, dq_ref, dk_ref, dv_ref, dkb_ref, dvb_ref,
                   iq_ref, ik_ref, ikb_ref, iw_ref):
    tm = x_ref.shape[0]
    h = (_rms(x_ref[...]) * g_ref[...]).astype(BF16)

    def proj(c0, width):
        return _dot(h, w_ref[:, c0:c0 + width])

    cos = jnp.concatenate([cos_ref[...]] * (W_RET // LANES), axis=1)
    sin = jnp.concatenate([sin_ref[...]] * (W_RET // LANES), axis=1)
    lane = lax.broadcasted_iota(I32, (tm, W_RET), 1)
    first_half = (lane & (DK_RET - 1)) < (DK_RET // 2)

    def rope(p):
        swapped = jnp.where(first_half, pltpu.roll(p, W_RET - DK_RET // 2, 1), pltpu.roll(p, DK_RET // 2, 1))
        return p * cos + swapped * sin

    rq_ref[...] = rope(proj(0, 512)).astype(BF16)
    rk_ref[...] = rope(proj(512, 512)) * (DK_RET ** -0.5)
    rv_ref[...] = proj(1024, 512).astype(BF16)
    rg_ref[...] = proj(1536, 512)
    dq_ref[...] = (proj(2048, 512) * (DH_DSA ** -0.5)).astype(BF16)
    dk = proj(2560, 512)
    dk_ref[...] = dk
    dkb_ref[...] = dk.astype(BF16)
    dv = proj(3072, 512)
    dv_ref[...] = dv
    dvb_ref[...] = dv.astype(BF16)
    iq_ref[...] = (proj(3584, 512) * (D_IDX ** -0.5)).astype(BF16)
    tail = proj(4096, LANES)
    ik = tail[:, :D_IDX]
    ik_ref[...] = ik
    ikb_ref[...] = ik.astype(BF16)
    iw_ref[...] = tail[:, D_IDX:D_IDX + H_IDX] * (H_IDX ** -0.5)


def _in_proj(x2d, g, w_pad, cos_t, sin_t, tm):
    n, d = x2d.shape
    nt = n // tm
    ntab = cos_t.shape[0] // tm
    row = lambda i: (i, 0)
    const = lambda i: (0, 0)
    tab = lambda i: (i % ntab, 0)
    wide = lambda dt: jax.ShapeDtypeStruct((n, 512), dt)
    out_shape = (wide(BF16), wide(F32), wide(BF16), wide(F32), wide(BF16), wide(F32), wide(F32),
                 wide(BF16), wide(BF16), wide(BF16),
                 jax.ShapeDtypeStruct((n, D_IDX), F32), jax.ShapeDtypeStruct((n, D_IDX), BF16),
                 jax.ShapeDtypeStruct((n, H_IDX), F32))
    out_specs = tuple([pl.BlockSpec((tm, 512), row)] * 10
                      + [pl.BlockSpec((tm, D_IDX), row), pl.BlockSpec((tm, D_IDX), row),
                         pl.BlockSpec((tm, H_IDX), row)])
    return pl.pallas_call(
        _inproj_kernel,
        grid=(nt,),
        in_specs=[pl.BlockSpec((tm, d), row), pl.BlockSpec((1, d), const),
                  pl.BlockSpec(w_pad.shape, const),
                  pl.BlockSpec((tm, LANES), tab), pl.BlockSpec((tm, LANES), tab)],
        out_specs=out_specs,
        out_shape=out_shape,
        compiler_params=pltpu.CompilerParams(dimension_semantics=("arbitrary",),
                                             vmem_limit_bytes=VMEM_LIMIT_BYTES),
        name="in_proj",
    )(x2d, g, w_pad, cos_t, sin_t)


def _ret_kernel(q_ref, k_ref, v_ref, g_ref, s0_ref, dec_ref, xi_ref, zeta_ref, gn_ref,
                y_ref, sout_ref, s_scr, *, gamma_c):
    c = pl.program_id(1)

    @pl.when(c == 0)
    def _():
        s_scr[...] = s0_ref[0]

    q = q_ref[...]
    k = k_ref[...]
    v = v_ref[...]
    kb = k.astype(BF16)
    kz = (k * zeta_ref[...]).astype(BF16)
    xi = xi_ref[...]
    normed = []
    for h in range(H_RET):
        sl = slice(DK_RET * h, DK_RET * (h + 1))
        qh, kh, vh, kzh = q[:, sl], kb[:, sl], v[:, sl], kz[:, sl]
        state = s_scr[h]
        scores = _dot_nt(qh, kh) * dec_ref[h]
        inner = _dot(scores.astype(BF16), vh)
        cross = _dot(qh, state.astype(BF16)) * xi[:, sl]
        o = inner + cross
        s_scr[h] = gamma_c[h] * state + _dot_tn(kzh, vh)
        mu = jnp.mean(o, axis=-1, keepdims=True)
        dev = o - mu
        var = jnp.mean(dev * dev, axis=-1, keepdims=True)
        normed.append(dev * lax.rsqrt(var + EPS))
    on = jnp.concatenate(normed, axis=1) * gn_ref[...]
    y_ref[...] = (_silu(g_ref[...]) * on).astype(BF16)

    @pl.when(c == pl.num_programs(1) - 1)
    def _():
        sout_ref[0] = s_scr[...]


def _retention_constants(c):
    lg = np.log1p(-np.exp2(-5.0 - np.arange(H_RET, dtype=np.float64)))
    idx = np.arange(c, dtype=np.float64)
    diff = idx[:, None] - idx[None, :]
    decay = np.where(diff[None] >= 0, np.exp(lg[:, None, None] * np.maximum(diff, 0.0)[None]), 0.0)
    xi = np.exp(lg[None, :] * (idx + 1.0)[:, None])
    zeta = np.exp(lg[None, :] * (c - 1.0 - idx)[:, None])
    gamma_c = tuple(float(np.float32(g)) for g in np.exp(lg * c))
    rep = lambda a: np.repeat(a, DK_RET, axis=1).astype(np.float32)
    return decay.astype(np.float32), rep(xi), rep(zeta), gamma_c


def _retention(rq, rk, rv, rg, s0, gn, batch, t):
    c = min(2 * CHUNK, t)
    nc = t // c
    decay, xi, zeta, gamma_c = _retention_constants(c)
    blk = lambda b, j: (b * nc + j, 0)
    const2 = lambda b, j: (0, 0)
    const3 = lambda b, j: (0, 0, 0)
    st = lambda b, j: (b, 0, 0, 0)
    n = batch * t
    return pl.pallas_call(
        functools.partial(_ret_kernel, gamma_c=gamma_c),
        grid=(batch, nc),
        in_specs=[pl.BlockSpec((c, W_RET), blk)] * 4
                 + [pl.BlockSpec((1, H_RET, DK_RET, DV_RET), st),
                    pl.BlockSpec((H_RET, c, c), const3),
                    pl.BlockSpec((c, W_RET), const2), pl.BlockSpec((c, W_RET), const2),
                    pl.BlockSpec((1, W_RET), const2)],
        out_specs=(pl.BlockSpec((c, W_RET), blk), pl.BlockSpec((1, H_RET, DK_RET, DV_RET), st)),
        out_shape=(jax.ShapeDtypeStruct((n, W_RET), BF16),
                   jax.ShapeDtypeStruct((batch, H_RET, DK_RET, DV_RET), F32)),
        scratch_shapes=[pltpu.VMEM((H_RET, DK_RET, DV_RET), F32)],
        compiler_params=pltpu.CompilerParams(dimension_semantics=("arbitrary", "arbitrary"),
                                             vmem_limit_bytes=VMEM_LIMIT_BYTES),
        name="retention",
    )(rq, rk, rv, rg, s0, jnp.asarray(decay), jnp.asarray(xi), jnp.asarray(zeta), gn)


def _sortable_key(score):
    bits = pltpu.bitcast(score + 0.0, I32)
    return jnp.where(bits < 0, bits ^ jnp.int32(0x7FFFFFFF), bits)


def _log2(n):
    assert n > 0 and n & (n - 1) == 0
    return n.bit_length() - 1


def _dsa_kernel(*refs, segs, tq, topk, q_pos0):
    nseg = len(segs)
    iq_ref, dq_ref, iw_ref = refs[:3]
    seg_refs = [refs[3 + 3 * s: 6 + 3 * s] for s in range(nseg)]
    o_ref = refs[3 + 3 * nseg]
    key_scr, qi_scr, qm_scr, w_scr, m_scr, l_scr, acc_scr = refs[4 + 3 * nseg:]

    chunk_shift = _log2(CHUNK)
    qb = pl.program_id(1)
    q_first = q_pos0 + qb * tq
    q_last_chunk = lax.shift_right_logical(q_first + tq - 1, chunk_shift)
    row = lax.broadcasted_iota(I32, (tq, LANES), 0)
    lane = lax.broadcasted_iota(I32, (tq, LANES), 1)
    q_chunk = lax.shift_right_logical(q_first + row, chunk_shift)

    iq = iq_ref[...]
    dq = dq_ref[...]
    iw = iw_ref[...]
    low_half = lane < DH_DSA
    for h in range(H_IDX):
        qi_scr[h] = iq[:, D_IDX * h:D_IDX * (h + 1)]
        w_scr[h] = jnp.broadcast_to(iw[:, h:h + 1], (tq, LANES))
    for h in range(H_DSA):
        pair = dq[:, LANES * (h // 2):LANES * (h // 2 + 1)]
        keep = low_half if h % 2 == 0 else jnp.logical_not(low_half)
        qm_scr[h] = jnp.where(keep, pair, jnp.zeros_like(pair))
    m_scr[...] = jnp.full(m_scr.shape, NEG_BIG, F32)
    l_scr[...] = jnp.zeros(l_scr.shape, F32)
    acc_scr[...] = jnp.zeros(acc_scr.shape, F32)

    def blocks_needed(seg):
        valid, blk, _, pos0 = seg
        adm = jnp.clip((q_last_chunk + 1) * CHUNK - pos0, 0, valid)
        return lax.shift_right_logical(adm + blk - 1, _log2(blk))

    def wide(a, blk):
        return jnp.concatenate([a] * (blk // LANES), axis=1) if blk > LANES else a

    def columns(seg, j):
        _, blk, col0, _ = seg
        return pl.ds(pl.multiple_of(col0 + j * blk, LANES), blk)

    for seg, (kidx_ref, _, _) in zip(segs, seg_refs):
        valid, blk, col0, pos0 = seg

        def score_block(j, carry, seg=seg, kidx_ref=kidx_ref, valid=valid, blk=blk, pos0=pos0):
            kidx = kidx_ref[0, pl.ds(pl.multiple_of(j * blk, blk), blk), :].astype(BF16)
            tot = jnp.zeros((tq, blk), F32)
            for h in range(H_IDX):
                tot = tot + jnp.maximum(_dot_nt(qi_scr[h], kidx), 0.0) * wide(w_scr[h], blk)
            idx = j * blk + jnp.concatenate([lane + LANES * r for r in range(blk // LANES)], axis=1)
            ok = (lax.shift_right_logical(pos0 + idx, chunk_shift) <= wide(q_chunk, blk)) & (idx < valid)
            key_scr[:, columns(seg, j)] = jnp.where(ok, _sortable_key(tot), jnp.int32(INT_MIN))
            return carry

        lax.fori_loop(0, blocks_needed(seg), score_block, 0)

    def count_rows(hit_fn):
        cnt = jnp.zeros((tq, LANES), I32)
        for seg in segs:
            _, blk, col0, _ = seg

            def body(j, cnt, seg=seg, blk=blk, col0=col0):
                kb = key_scr[:, columns(seg, j)]
                for r in range(blk // LANES):
                    hit = hit_fn(kb[:, LANES * r:LANES * (r + 1)], col0 + j * blk + LANES * r)
                    cnt = cnt + jnp.where(hit, 1, 0)
                return cnt

            cnt = lax.fori_loop(0, blocks_needed(seg), body, cnt)
        return jnp.broadcast_to(jnp.sum(cnt, axis=1, keepdims=True), (tq, LANES))

    def count_ge(cand):
        return count_rows(lambda kb, c0: kb >= cand)

    total = jnp.zeros((tq, LANES), I32)
    for seg in segs:
        total = total + blocks_needed(seg) * seg[1]

    def bisect(i, carry):
        ans, n_ans = carry
        cand = jnp.where(i == 0, jnp.zeros_like(ans), ans | jnp.left_shift(jnp.int32(1), 31 - i))
        n = count_ge(cand)
        take = n >= topk
        return jnp.where(take, cand, ans), jnp.where(take, n, n_ans)

    thr, n_ge = lax.fori_loop(0, 32, bisect, (jnp.full((tq, LANES), INT_MIN, I32), total))

    need = (n_ge > topk) & (thr > INT_MIN)

    @pl.when(jnp.max(jnp.where(need, 1, 0)) > 0)
    def _():
        quota = topk - count_ge(thr + 1)
        nbits = int(key_scr.shape[1]).bit_length()

        def index_bisect(i, last):
            cand = last | jnp.left_shift(jnp.int32(1), nbits - 1 - i)
            below = count_rows(lambda kb, c0: (kb == thr) & (c0 + lane < cand))
            return jnp.where(below < quota, cand, last)

        last = lax.fori_loop(0, nbits, index_bisect, jnp.zeros((tq, LANES), I32))

        for seg in segs:
            _, blk, col0, _ = seg

            def drop(j, carry, blk=blk, col0=col0):
                for r in range(blk // LANES):
                    c0 = col0 + j * blk + LANES * r
                    cols = pl.ds(pl.multiple_of(c0, LANES), LANES)
                    kb = key_scr[:, cols]
                    cut = need & (kb == thr) & (c0 + lane > last)
                    key_scr[:, cols] = jnp.where(cut, jnp.int32(INT_MIN), kb)
                return carry

            lax.fori_loop(0, blocks_needed(seg), drop, 0)

    thr_sel = jnp.maximum(thr, INT_MIN + 1)

    for seg, (_, k_ref, v_ref) in zip(segs, seg_refs):
        _, blk, _, _ = seg

        def attend(j, carry, seg=seg, k_ref=k_ref, v_ref=v_ref, blk=blk):
            sel = key_scr[:, columns(seg, j)] >= wide(thr_sel, blk)
            rows = pl.ds(pl.multiple_of(j * blk, blk), blk)
            kblk = k_ref[0, rows, :].astype(BF16)
            vblk = v_ref[0, rows, :].astype(BF16)
            for h in range(H_DSA):
                lanes = slice(LANES * (h // 2), LANES * (h // 2 + 1))
                logit = jnp.where(sel, _dot_nt(qm_scr[h], kblk[:, lanes]), NEG_BIG)
                m_old = m_scr[h]
                m_new = jnp.maximum(m_old, jnp.max(logit, axis=1, keepdims=True))
                alpha = jnp.exp(m_old - m_new)
                p = jnp.exp(logit - m_new[:, :1])
                l_scr[h] = alpha * l_scr[h] + jnp.sum(p, axis=1, keepdims=True)
                acc_scr[h] = alpha * acc_scr[h] + _dot(p.astype(BF16), vblk[:, lanes])
                m_scr[h] = m_new
            return carry

        lax.fori_loop(0, blocks_needed(seg), attend, 0)

    outs = []
    for hp in range(H_DSA // 2):
        even = acc_scr[2 * hp] / l_scr[2 * hp]
        odd = acc_scr[2 * hp + 1] / l_scr[2 * hp + 1]
        outs.append(jnp.where(low_half, even, odd))
    o_ref[...] = jnp.concatenate(outs, axis=1).astype(BF16)


def _dsa(iq, dq, iw, key_sets, batch, t, past, tq):
    nq = t // tq
    ltot = sum(ks[0] for ks in key_sets)
    topk = min(TOPK_MAX, ltot // 4)
    segs = []
    col = 0
    pos = 0
    for valid, kidx, _, _ in key_sets:
        length = kidx.shape[1]
        blk = min(512, length)
        assert blk % LANES == 0 and length % blk == 0 and valid <= length
        segs.append((valid, blk, col, pos))
        col += length
        pos += valid
    assert topk <= col
    qrow = lambda b, i: (b * nq + i, 0)
    per_b = lambda b, i: (b, 0, 0)
    in_specs = [pl.BlockSpec((tq, 512), qrow), pl.BlockSpec((tq, 512), qrow), pl.BlockSpec((tq, H_IDX), qrow)]
    operands = [iq, dq, iw]
    for _, kidx, k, v in key_sets:
        length = kidx.shape[1]
        in_specs += [pl.BlockSpec((1, length, D_IDX), per_b), pl.BlockSpec((1, length, W_DSA), per_b),
                     pl.BlockSpec((1, length, W_DSA), per_b)]
        operands += [kidx, k, v]
    return pl.pallas_call(
        functools.partial(_dsa_kernel, segs=tuple(segs), tq=tq, topk=topk, q_pos0=past),
        grid=(batch, nq),
        in_specs=in_specs,
        out_specs=pl.BlockSpec((tq, W_DSA), qrow),
        out_shape=jax.ShapeDtypeStruct((batch * t, W_DSA), BF16),
        scratch_shapes=[pltpu.VMEM((tq, col), I32),
                        pltpu.VMEM((H_IDX, tq, D_IDX), BF16),
                        pltpu.VMEM((H_DSA, tq, LANES), BF16),
                        pltpu.VMEM((H_IDX, tq, LANES), F32),
                        pltpu.VMEM((H_DSA, tq, LANES), F32),
                        pltpu.VMEM((H_DSA, tq, LANES), F32),
                        pltpu.VMEM((H_DSA, tq, LANES), F32)],
        compiler_params=pltpu.CompilerParams(dimension_semantics=("arbitrary", "arbitrary"),
                                             vmem_limit_bytes=VMEM_LIMIT_BYTES),
        name="dsa",
    )(*operands)


def _ffn_kernel(x_ref, yr_ref, od_ref, wo_ref, g2_ref, wup_ref, wg_ref, wd_ref, cw_ref, cb_ref, cs_ref, gf_ref,
                y_ref, cnew_ref, carry_scr, *, fc):
    t = pl.program_id(1)
    tm = x_ref.shape[0]
    d_ff = wup_ref.shape[1]

    @pl.when(t == 0)
    def _():
        carry_scr[...] = cs_ref[0]

    x1 = x_ref[...] + _dot(yr_ref[...], wo_ref[:W_RET, :]) + _dot(od_ref[...], wo_ref[W_RET:, :])
    h2 = (_rms(x1) * g2_ref[...]).astype(BF16)
    row = lax.broadcasted_iota(I32, (tm, fc), 0)
    acc = jnp.zeros(x1.shape, F32)
    for c0 in range(0, d_ff, fc):
        cols = slice(c0, c0 + fc)
        a = _dot(h2, wup_ref[:, cols])
        u = _dot(h2, wg_ref[:, cols])
        prev2 = carry_scr[0:1, cols]
        prev1 = carry_scr[1:2, cols]
        a1 = jnp.where(row == 0, prev1, pltpu.roll(a, 1, 0))
        a2 = jnp.where(row == 0, prev2, jnp.where(row == 1, prev1, pltpu.roll(a, 2, 0)))
        cw = cw_ref[:, cols]
        conv = cb_ref[:, cols] + a2 * cw[0:1] + a1 * cw[1:2] + a * cw[2:3]
        carry_scr[:, cols] = a[tm - 2:tm, :]
        acc = acc + _dot((_silu(conv) * u).astype(BF16), wd_ref[cols, :])
    y_ref[...] = _rms(x1 + acc) * gf_ref[...]

    @pl.when(t == pl.num_programs(1) - 1)
    def _():
        cnew_ref[0] = carry_scr[...]


def _out_ffn(x2d, y_ret, o_dsa, w_out, g2, w_up, w_gate, w_down, conv_w, conv_b, conv_state, g_final,
             batch, t, tm):
    n, d = x2d.shape
    d_ff = w_up.shape[1]
    nt = t // tm
    fc = 256
    assert d_ff % fc == 0
    row = lambda b, i: (b * nt + i, 0)
    const = lambda b, i: (0, 0)
    per_b = lambda b, i: (b, 0, 0)
    return pl.pallas_call(
        functools.partial(_ffn_kernel, fc=fc),
        grid=(batch, nt),
        in_specs=[pl.BlockSpec((tm, d), row), pl.BlockSpec((tm, W_RET), row), pl.BlockSpec((tm, W_DSA), row),
                  pl.BlockSpec(w_out.shape, const), pl.BlockSpec((1, d), const),
                  pl.BlockSpec(w_up.shape, const), pl.BlockSpec(w_gate.shape, const),
                  pl.BlockSpec(w_down.shape, const), pl.BlockSpec(conv_w.shape, const),
                  pl.BlockSpec((1, d_ff), const), pl.BlockSpec((1, CONV_W - 1, d_ff), per_b),
                  pl.BlockSpec((1, d), const)],
        out_specs=(pl.BlockSpec((tm, d), row), pl.BlockSpec((1, CONV_W - 1, d_ff), per_b)),
        out_shape=(jax.ShapeDtypeStruct((n, d), F32), jax.ShapeDtypeStruct((batch, CONV_W - 1, d_ff), F32)),
        scratch_shapes=[pltpu.VMEM((CONV_W - 1, d_ff), F32)],
        compiler_params=pltpu.CompilerParams(dimension_semantics=("arbitrary", "arbitrary"),
                                             vmem_limit_bytes=VMEM_LIMIT_BYTES),
        name="out_ffn",
    )(x2d, y_ret, o_dsa, w_out, g2, w_up, w_gate, w_down, conv_w, conv_b, conv_state, g_final)


def _rope_tables(past, t, rows):
    half = DK_RET // 2
    inv_freq = ROPE_BASE ** (-jnp.arange(half, dtype=F32) / half)
    pos = past + jnp.arange(t, dtype=jnp.int32)
    ang = pos.astype(F32)[:, None] * inv_freq[None, :]
    cos, sin = jnp.cos(ang), jnp.sin(ang)
    cos_t = jnp.concatenate([cos, cos, cos, cos], axis=1)
    sin_t = jnp.concatenate([-sin, sin, -sin, sin], axis=1)
    if t < rows:
        cos_t = jnp.tile(cos_t, (rows // t, 1))
        sin_t = jnp.tile(sin_t, (rows // t, 1))
    return cos_t, sin_t


def _trunk_layer(x, ret_state, past_kv, conv_state, weights, g_final):
    w_in_pad, w_out, g_mix, g_gn, g_ffn, w_up, w_gate, w_down, conv_w, conv_b = weights
    batch, t, d = x.shape
    n = batch * t
    past = 0 if past_kv is None else past_kv[0].shape[1]
    x2d = x.reshape(n, d)

    tm_in = min(512, n)
    cos_t, sin_t = _rope_tables(past, t, tm_in)
    (rq, rk, rv, rg, dq, dk, dv, dkb, dvb, iq, ik, ikb, iw) = _in_proj(x2d, g_mix, w_in_pad, cos_t, sin_t, tm_in)

    y_ret, ret_new = _retention(rq, rk, rv, rg, ret_state, g_gn, batch, t)

    if past_kv is None:
        key_sets = [(t, ikb.reshape(batch, t, D_IDX), dkb.reshape(batch, t, W_DSA), dvb.reshape(batch, t, W_DSA))]
    else:
        past_k, past_v, past_ki = past_kv
        rows = -(-t // LANES) * LANES
        padded = lambda a, w: jnp.pad(a.reshape(batch, t, w), ((0, 0), (0, rows - t), (0, 0)))
        key_sets = [(past, past_ki, past_k.reshape(batch, past, W_DSA), past_v.reshape(batch, past, W_DSA)),
                    (t, padded(ikb, D_IDX), padded(dkb, W_DSA), padded(dvb, W_DSA))]
    o_dsa = _dsa(iq, dq, iw, key_sets, batch, t, past, min(256, t))

    y, conv_new = _out_ffn(x2d, y_ret, o_dsa, w_out, g_ffn, w_up, w_gate, w_down, conv_w, conv_b,
                           conv_state, g_final, batch, t, min(512, t))
    return (y.reshape(batch, t, d), dk.reshape(batch, t, H_DSA, DH_DSA), dv.reshape(batch, t, H_DSA, DH_DSA),
            ik.reshape(batch, t, D_IDX), ret_new, conv_new)


def kernel(x_prompt, x_sample, cache_dsa_k, cache_dsa_v, cache_idx_k, state_ret, state_ffn_conv, w_in, w_out,
           g_norm_mix, g_gn_ret, g_norm_ffn, w_up, w_gate, w_down, conv_w, conv_b, g_norm_final):
    depth = w_in.shape[0]
    assert depth == 1, "the final norm is fused into the layer's last kernel"
    bp = x_prompt.shape[0]
    d_ff = w_up.shape[-1]
    l = 0
    p_in = w_in.shape[-1]
    pad = (-p_in) % LANES
    weights = (jnp.pad(w_in[l], ((0, 0), (0, pad))).astype(BF16), w_out[l].astype(BF16),
               g_norm_mix[l][None, :], g_gn_ret[l][None, :], g_norm_ffn[l][None, :],
               w_up[l].astype(BF16), w_gate[l].astype(BF16), w_down[l].astype(BF16),
               conv_w[l], conv_b[l][None, :])
    g_final = g_norm_final[None, :]

    yp, pk, pv, pki, pret, pconv = _trunk_layer(
        x_prompt, jnp.zeros((bp, H_RET, DK_RET, DV_RET), F32), None,
        jnp.zeros((bp, CONV_W - 1, d_ff), F32), weights, g_final)
    ys, sk, sv, ski, sret, sconv = _trunk_layer(
        x_sample, state_ret[l], (cache_dsa_k[l], cache_dsa_v[l], cache_idx_k[l]),
        state_ffn_conv[l], weights, g_final)
    st = lambda a: a[None]
    return (yp, ys, st(pk), st(pv), st(pki), st(pret), st(pconv),
            st(sk), st(sv), st(ski), st(sret), st(sconv))
```

```python
import functools

import numpy as np
import jax
import jax.numpy as jnp
from jax import lax
from jax.experimental import pallas as pl
from jax.experimental.pallas import tpu as pltpu

F32 = jnp.float32
BF16 = jnp.bfloat16
I32 = jnp.int32
I16 = jnp.int16

CHUNK = 64
H_RET = 8
DK_RET = 64
DV_RET = 64
H_DSA = 8
DH_DSA = 64
H_IDX = 8
D_IDX = 64
TOPK_MAX = 256
CONV_W = 3
ROPE_BASE = 10000.0
EPS = 1e-6
W_RET = H_RET * DV_RET
W_DSA = H_DSA * DH_DSA

LANES = 128
SUBLANES = 8
HALF_ROWS = 16
DEN_ROWS = 16
VMEM_LIMIT_BYTES = 56 * 1024 * 1024
INT_MIN = -(2 ** 31)
NEG_BIG = -1e30
LOG2E = 1.4426950408889634
SAFE_DENOM_MIN = 2.0 ** -100
SAFE_DENOM_MAX = 2.0 ** 100


def _dot(a, b):
    return jnp.dot(a, b, preferred_element_type=F32)


def _dot_nt(a, b):
    return lax.dot_general(a, b, (((1,), (1,)), ((), ())), preferred_element_type=F32)


def _dot_tn(a, b):
    return lax.dot_general(a, b, (((0,), (0,)), ((), ())), preferred_element_type=F32)


def _rms(x):
    return x * lax.rsqrt(jnp.mean(x * x, axis=-1, keepdims=True) + EPS)


def _silu(x):
    return x / (1.0 + jnp.exp(-x))


def _inproj_kernel(x_ref, g_ref, w_ref, cos_ref, sin_ref,
                   rq_ref, rk_ref, rv_ref, rg_ref, dq_ref, dk_ref, dv_ref, dkb_ref, dvb_ref,
                   iq_ref, ik_ref, ikb_ref, iw_ref):
    tm = x_ref.shape[0]
    h = (_rms(x_ref[...]) * g_ref[...]).astype(BF16)

    def proj(c0, width):
        return _dot(h, w_ref[:, c0:c0 + width])

    cos = jnp.concatenate([cos_ref[...]] * (W_RET // LANES), axis=1)
    sin = jnp.concatenate([sin_ref[...]] * (W_RET // LANES), axis=1)
    lane = lax.broadcasted_iota(I32, (tm, W_RET), 1)
    first_half = (lane & (DK_RET - 1)) < (DK_RET // 2)

    def rope(p):
        swapped = jnp.where(first_half, pltpu.roll(p, W_RET - DK_RET // 2, 1), pltpu.roll(p, DK_RET // 2, 1))
        return p * cos + swapped * sin

    rq_ref[...] = rope(proj(0, 512)).astype(BF16)
    rk_ref[...] = rope(proj(512, 512)) * (DK_RET ** -0.5)
    rv_ref[...] = proj(1024, 512).astype(BF16)
    rg_ref[...] = proj(1536, 512)
    dq_ref[...] = (proj(2048, 512) * (DH_DSA ** -0.5 * LOG2E)).astype(BF16)
    dk = proj(2560, 512)
    dk_ref[...] = dk
    dkb_ref[...] = dk.astype(BF16)
    dv = proj(3072, 512)
    dv_ref[...] = dv
    dvb_ref[...] = dv.astype(BF16)
    iq_ref[...] = (proj(3584, 512) * (D_IDX ** -0.5)).astype(BF16)
    tail = proj(4096, LANES)
    ik = tail[:, :D_IDX]
    ik_ref[...] = ik
    ikb_ref[...] = ik.astype(BF16)
    iw_ref[...] = tail[:, D_IDX:D_IDX + H_IDX] * (H_IDX ** -0.5)


def _in_proj(x2d, g, w_pad, cos_t, sin_t, tm):
    n, d = x2d.shape
    nt = n // tm
    ntab = cos_t.shape[0] // tm
    row = lambda i: (i, 0)
    const = lambda i: (0, 0)
    tab = lambda i: (i % ntab, 0)
    wide = lambda dt: jax.ShapeDtypeStruct((n, 512), dt)
    out_shape = (wide(BF16), wide(F32), wide(BF16), wide(F32), wide(BF16), wide(F32), wide(F32),
                 wide(BF16), wide(BF16), wide(BF16),
                 jax.ShapeDtypeStruct((n, D_IDX), F32), jax.ShapeDtypeStruct((n, D_IDX), BF16),
                 jax.ShapeDtypeStruct((n, H_IDX), F32))
    out_specs = tuple([pl.BlockSpec((tm, 512), row)] * 10
                      + [pl.BlockSpec((tm, D_IDX), row), pl.BlockSpec((tm, D_IDX), row),
                         pl.BlockSpec((tm, H_IDX), row)])
    return pl.pallas_call(
        _inproj_kernel,
        grid=(nt,),
        in_specs=[pl.BlockSpec((tm, d), row), pl.BlockSpec((1, d), const),
                  pl.BlockSpec(w_pad.shape, const),
                  pl.BlockSpec((tm, LANES), tab), pl.BlockSpec((tm, LANES), tab)],
        out_specs=out_specs,
        out_shape=out_shape,
        compiler_params=pltpu.CompilerParams(dimension_semantics=("arbitrary",),
                                             vmem_limit_bytes=VMEM_LIMIT_BYTES),
        name="in_proj",
    )(x2d, g, w_pad, cos_t, sin_t)


def _ret_kernel(q_ref, k_ref, v_ref, g_ref, s0_ref, dec_ref, xi_ref, zeta_ref, gn_ref,
                y_ref, sout_ref, s_scr, *, gamma_c):
    c = pl.program_id(1)

    @pl.when(c == 0)
    def _():
        s_scr[...] = s0_ref[0]

    q = q_ref[...]
    k = k_ref[...]
    v = v_ref[...]
    kb = k.astype(BF16)
    kz = (k * zeta_ref[...]).astype(BF16)
    xi = xi_ref[...]
    normed = []
    for h in range(H_RET):
        sl = slice(DK_RET * h, DK_RET * (h + 1))
        qh, kh, vh, kzh = q[:, sl], kb[:, sl], v[:, sl], kz[:, sl]
        state = s_scr[h]
        scores = _dot_nt(qh, kh) * dec_ref[h]
        inner = _dot(scores.astype(BF16), vh)
        cross = _dot(qh, state.astype(BF16)) * xi[:, sl]
        o = inner + cross
        s_scr[h] = gamma_c[h] * state + _dot_tn(kzh, vh)
        mu = jnp.mean(o, axis=-1, keepdims=True)
        dev = o - mu
        var = jnp.mean(dev * dev, axis=-1, keepdims=True)
        normed.append(dev * lax.rsqrt(var + EPS))
    on = jnp.concatenate(normed, axis=1) * gn_ref[...]
    y_ref[...] = (_silu(g_ref[...]) * on).astype(BF16)

    @pl.when(c == pl.num_programs(1) - 1)
    def _():
        sout_ref[0] = s_scr[...]


def _retention_constants(c):
    lg = np.log1p(-np.exp2(-5.0 - np.arange(H_RET, dtype=np.float64)))
    idx = np.arange(c, dtype=np.float64)
    diff = idx[:, None] - idx[None, :]
    decay = np.where(diff[None] >= 0, np.exp(lg[:, None, None] * np.maximum(diff, 0.0)[None]), 0.0)
    xi = np.exp(lg[None, :] * (idx + 1.0)[:, None])
    zeta = np.exp(lg[None, :] * (c - 1.0 - idx)[:, None])
    gamma_c = tuple(float(np.float32(g)) for g in np.exp(lg * c))
    rep = lambda a: np.repeat(a, DK_RET, axis=1).astype(np.float32)
    return decay.astype(np.float32), rep(xi), rep(zeta), gamma_c


def _retention(rq, rk, rv, rg, s0, gn, batch, t):
    c = min(2 * CHUNK, t)
    nc = t // c
    decay, xi, zeta, gamma_c = _retention_constants(c)
    blk = lambda b, j: (b * nc + j, 0)
    const2 = lambda b, j: (0, 0)
    const3 = lambda b, j: (0, 0, 0)
    st = lambda b, j: (b, 0, 0, 0)
    n = batch * t
    return pl.pallas_call(
        functools.partial(_ret_kernel, gamma_c=gamma_c),
        grid=(batch, nc),
        in_specs=[pl.BlockSpec((c, W_RET), blk)] * 4
                 + [pl.BlockSpec((1, H_RET, DK_RET, DV_RET), st),
                    pl.BlockSpec((H_RET, c, c), const3),
                    pl.BlockSpec((c, W_RET), const2), pl.BlockSpec((c, W_RET), const2),
                    pl.BlockSpec((1, W_RET), const2)],
        out_specs=(pl.BlockSpec((c, W_RET), blk), pl.BlockSpec((1, H_RET, DK_RET, DV_RET), st)),
        out_shape=(jax.ShapeDtypeStruct((n, W_RET), BF16),
                   jax.ShapeDtypeStruct((batch, H_RET, DK_RET, DV_RET), F32)),
        scratch_shapes=[pltpu.VMEM((H_RET, DK_RET, DV_RET), F32)],
        compiler_params=pltpu.CompilerParams(dimension_semantics=("arbitrary", "arbitrary"),
                                             vmem_limit_bytes=VMEM_LIMIT_BYTES),
        name="retention",
    )(rq, rk, rv, rg, s0, jnp.asarray(decay), jnp.asarray(xi), jnp.asarray(zeta), gn)


def _sortable_key(score):
    bits = pltpu.bitcast(score + 0.0, I32)
    return jnp.where(bits < 0, bits ^ jnp.int32(0x7FFFFFFF), bits)


def _log2(n):
    assert n > 0 and n & (n - 1) == 0
    return n.bit_length() - 1


def _dsa_kernel(*refs, segs, tq, topk, q_pos0):
    nseg = len(segs)
    iq_ref, dq_ref, iw_ref = refs[:3]
    seg_refs = [refs[3 + 3 * s: 6 + 3 * s] for s in range(nseg)]
    o_ref = refs[3 + 3 * nseg]
    key_scr, half_scr, qi_scr, qm_scr, w_scr, m_scr, acc_scr, kmax_scr = refs[4 + 3 * nseg:]

    chunk_shift = _log2(CHUNK)
    qb = pl.program_id(1)
    q_first = q_pos0 + qb * tq
    q_last_chunk = lax.shift_right_logical(q_first + tq - 1, chunk_shift)
    q_chunk = lax.shift_right_logical(q_first + lax.broadcasted_iota(I32, (1, tq), 1), chunk_shift)

    iq_t = iq_ref[...].astype(F32).T
    dq_t = dq_ref[...].astype(F32).T
    w_scr[...] = iw_ref[...].T
    pair_row = lax.broadcasted_iota(I32, (LANES, tq), 0)
    for h in range(H_IDX):
        qi_scr[h] = iq_t[D_IDX * h:D_IDX * (h + 1)].astype(BF16)
    for hp in range(H_DSA // 2):
        pair = dq_t[LANES * hp:LANES * (hp + 1)]
        qm_scr[hp] = jnp.concatenate([jnp.where(pair_row < DH_DSA, pair, 0.0),
                                      jnp.where(pair_row >= DH_DSA, pair, 0.0)], axis=1).astype(BF16)
    acc_scr[...] = jnp.zeros(acc_scr.shape, F32)

    def blocks_needed(seg):
        valid, blk, _, pos0 = seg
        adm = jnp.clip((q_last_chunk + 1) * CHUNK - pos0, 0, valid)
        return lax.shift_right_logical(adm + blk - 1, _log2(blk))

    def rows_of(seg, j):
        _, blk, row0, _ = seg
        return pl.ds(pl.multiple_of(row0 + j * blk, blk), blk)

    for seg, (kidx_ref, _, _) in zip(segs, seg_refs):
        valid, blk, _, pos0 = seg

        def score_block(j, carry, seg=seg, kidx_ref=kidx_ref, valid=valid, blk=blk, pos0=pos0):
            kidx = kidx_ref[0, pl.ds(pl.multiple_of(j * blk, blk), blk), :].astype(BF16)
            tot = jnp.zeros((blk, tq), F32)
            for h in range(H_IDX):
                tot = tot + jnp.maximum(_dot(kidx, qi_scr[h]), 0.0) * w_scr[h:h + 1, :]
            idx = j * blk + lax.broadcasted_iota(I32, (blk, tq), 0)
            ok = (lax.shift_right_logical(pos0 + idx, chunk_shift) <= q_chunk) & (idx < valid)
            key = jnp.where(ok, _sortable_key(tot), jnp.int32(INT_MIN))
            key_scr[rows_of(seg, j), :] = key
            half_scr[rows_of(seg, j), :] = lax.shift_right_arithmetic(key, 16).astype(I16)
            return carry

        lax.fori_loop(0, blocks_needed(seg), score_block, 0)

    def count_half(hit_fn):
        cnt = jnp.zeros((HALF_ROWS, tq), I16)
        for seg in segs:
            blk = seg[1]

            def body(j, cnt, seg=seg, blk=blk):
                one = jnp.where(hit_fn(half_scr[rows_of(seg, j), :]), jnp.int16(1), jnp.int16(0))
                for r in range(blk // HALF_ROWS):
                    cnt = cnt + one[HALF_ROWS * r:HALF_ROWS * (r + 1)]
                return cnt

            cnt = lax.fori_loop(0, blocks_needed(seg), body, cnt)
        return jnp.sum(cnt.astype(I32), axis=0, keepdims=True)

    def kth_largest_half(k, n_all):
        def step(i, carry):
            ans, n_ans = carry
            cand = jnp.where(i == 0, jnp.zeros_like(ans), ans | jnp.left_shift(jnp.int32(1), 15 - i))
            cand16 = cand.astype(I16)
            n = count_half(lambda hb: hb >= cand16)
            take = n >= k
            return jnp.where(take, cand, ans), jnp.where(take, n, n_ans)

        return lax.fori_loop(0, 16, step, (jnp.full((1, tq), -32768, I32), n_all))

    total = jnp.zeros((1, tq), I32)
    for seg in segs:
        total = total + blocks_needed(seg) * seg[1]

    hi, n_hi_ge = kth_largest_half(topk, total)
    hi16 = hi.astype(I16)
    n_hi_gt = count_half(lambda hb: hb > hi16)
    quota_lo = topk - n_hi_gt

    for seg in segs:
        def low_halves(j, carry, seg=seg):
            k32 = key_scr[rows_of(seg, j), :]
            lo = (k32 & 0xFFFF) - 32768
            tied_hi = lax.shift_right_arithmetic(k32, 16) == hi
            half_scr[rows_of(seg, j), :] = jnp.where(tied_hi, lo, -32768).astype(I16)
            return carry

        lax.fori_loop(0, blocks_needed(seg), low_halves, 0)

    lo, n_lo_ge = kth_largest_half(quota_lo, n_hi_ge - n_hi_gt)
    thr = lax.shift_left(hi, 16) | (lo + 32768)
    n_ge = n_hi_gt + n_lo_ge

    need = (n_ge > topk) & (thr > INT_MIN)

    @pl.when(jnp.max(jnp.where(need, 1, 0)) > 0)
    def _():
        def count_keys(hit_fn):
            cnt = jnp.zeros((SUBLANES, tq), I32)
            for seg in segs:
                blk, row0 = seg[1], seg[2]

                def body(j, cnt, seg=seg, blk=blk, row0=row0):
                    kb = key_scr[rows_of(seg, j), :]
                    ridx = row0 + j * blk + lax.broadcasted_iota(I32, (blk, tq), 0)
                    one = jnp.where(hit_fn(kb, ridx), 1, 0)
                    for r in range(blk // SUBLANES):
                        cnt = cnt + one[SUBLANES * r:SUBLANES * (r + 1)]
                    return cnt

                cnt = lax.fori_loop(0, blocks_needed(seg), body, cnt)
            return jnp.sum(cnt, axis=0, keepdims=True)

        quota = topk - count_keys(lambda kb, ridx: kb > thr)
        nbits = int(key_scr.shape[0]).bit_length()

        def index_bisect(i, last):
            cand = last | jnp.left_shift(jnp.int32(1), nbits - 1 - i)
            below = count_keys(lambda kb, ridx: (kb == thr) & (ridx < cand))
            return jnp.where(below < quota, cand, last)

        last = lax.fori_loop(0, nbits, index_bisect, jnp.zeros((1, tq), I32))

        for seg in segs:
            blk, row0 = seg[1], seg[2]

            def drop(j, carry, seg=seg, blk=blk, row0=row0):
                kb = key_scr[rows_of(seg, j), :]
                ridx = row0 + j * blk + lax.broadcasted_iota(I32, (blk, tq), 0)
                cut = need & (kb == thr) & (ridx > last)
                key_scr[rows_of(seg, j), :] = jnp.where(cut, jnp.int32(INT_MIN), kb)
                return carry

            lax.fori_loop(0, blocks_needed(seg), drop, 0)

    thr_sel = jnp.maximum(thr, INT_MIN + 1)

    def attend_all(body):
        for seg, (_, k_ref, v_ref) in zip(segs, seg_refs):
            blk = seg[1]

            def step(j, carry, seg=seg, k_ref=k_ref, v_ref=v_ref, blk=blk):
                bias = jnp.where(key_scr[rows_of(seg, j), :] >= thr_sel, 0.0, NEG_BIG)
                bias = jnp.concatenate([bias, bias], axis=1)
                rows = pl.ds(pl.multiple_of(j * blk, blk), blk)
                kblk = k_ref[0, rows, :].astype(BF16)
                vblk = v_ref[0, rows, :].astype(F32)
                ones = jnp.ones((DEN_ROWS, blk), BF16)
                for hp in range(H_DSA // 2):
                    lanes = slice(LANES * hp, LANES * (hp + 1))
                    s = _dot(kblk[:, lanes], qm_scr[hp]) + bias
                    v_aug = jnp.concatenate([vblk[:, lanes].T.astype(BF16), ones], axis=0)
                    body(hp, s, v_aug)
                return carry

            lax.fori_loop(0, blocks_needed(seg), step, 0)

    @pl.when(qb == 0)
    def _():
        feat = lax.shift_right_logical(lax.broadcasted_iota(I32, (W_DSA, LANES), 0), _log2(DH_DSA))
        head_sum = jnp.where(feat == lax.broadcasted_iota(I32, (W_DSA, LANES), 1), 1.0, 0.0).astype(BF16)
        best = jnp.zeros((1, LANES), F32)
        for seg, (_, k_ref, _) in zip(segs, seg_refs):
            blk = seg[1]

            def norms(j, best, k_ref=k_ref, blk=blk):
                kb = k_ref[0, pl.ds(pl.multiple_of(j * blk, blk), blk), :].astype(F32)
                sq = _dot((kb * kb).astype(BF16), head_sum)
                return jnp.maximum(best, jnp.max(sq, axis=0, keepdims=True))

            best = lax.fori_loop(0, k_ref.shape[1] // blk, norms, best)
        knorm = jnp.sqrt(best)
        for h in range(H_DSA):
            kmax_scr[h // 2, :, tq * (h % 2):tq * (h % 2 + 1)] = jnp.broadcast_to(knorm[:, h:h + 1], (1, tq))

    for hp in range(H_DSA // 2):
        qp = qm_scr[hp].astype(F32)
        m_scr[hp] = jnp.sqrt(jnp.sum(qp * qp, axis=0, keepdims=True)) * kmax_scr[hp]

    def fast_pair(hp, s, v_aug):
        acc_scr[hp] = acc_scr[hp] + _dot(v_aug, jnp.exp2(s - m_scr[hp]).astype(BF16))

    attend_all(fast_pair)

    bad = jnp.zeros((1, 2 * tq), I32)
    for hp in range(H_DSA // 2):
        den = acc_scr[hp][LANES:LANES + 1]
        bad = bad | jnp.where((den >= SAFE_DENOM_MIN) & (den <= SAFE_DENOM_MAX), 0, 1)

    @pl.when(jnp.max(bad) > 0)
    def _():
        m_scr[...] = jnp.full(m_scr.shape, NEG_BIG, F32)
        acc_scr[...] = jnp.zeros(acc_scr.shape, F32)

        def online_pair(hp, s, v_aug):
            m_old = m_scr[hp]
            m_new = jnp.maximum(m_old, jnp.max(s, axis=0, keepdims=True))
            acc_scr[hp] = (jnp.exp2(m_old - m_new) * acc_scr[hp]
                           + _dot(v_aug, jnp.exp2(s - m_new).astype(BF16)))
            m_scr[hp] = m_new

        attend_all(online_pair)

    for hp in range(H_DSA // 2):
        acc = acc_scr[hp]
        den = acc[LANES:LANES + 1]
        pair_t = jnp.concatenate([acc[:DH_DSA, :tq] / den[:, :tq], acc[DH_DSA:LANES, tq:] / den[:, tq:]], axis=0)
        o_ref[:, LANES * hp:LANES * (hp + 1)] = pair_t.T.astype(BF16)


def _dsa(iq, dq, iw, key_sets, batch, t, past, tq):
    nq = t // tq
    ltot = sum(ks[0] for ks in key_sets)
    topk = min(TOPK_MAX, ltot // 4)
    segs = []
    row = 0
    pos = 0
    for valid, kidx, _, _ in key_sets:
        length = kidx.shape[1]
        blk = min(512, length)
        assert blk % LANES == 0 and length % blk == 0 and valid <= length
        segs.append((valid, blk, row, pos))
        row += length
        pos += valid
    assert topk <= row and tq % LANES == 0
    qrow = lambda b, i: (b * nq + i, 0)
    per_b = lambda b, i: (b, 0, 0)
    in_specs = [pl.BlockSpec((tq, 512), qrow), pl.BlockSpec((tq, 512), qrow), pl.BlockSpec((tq, H_IDX), qrow)]
    operands = [iq, dq, iw]
    for _, kidx, k, v in key_sets:
        length = kidx.shape[1]
        in_specs += [pl.BlockSpec((1, length, D_IDX), per_b), pl.BlockSpec((1, length, W_DSA), per_b),
                     pl.BlockSpec((1, length, W_DSA), per_b)]
        operands += [kidx, k, v]
    return pl.pallas_call(
        functools.partial(_dsa_kernel, segs=tuple(segs), tq=tq, topk=topk, q_pos0=past),
        grid=(batch, nq),
        in_specs=in_specs,
        out_specs=pl.BlockSpec((tq, W_DSA), qrow),
        out_shape=jax.ShapeDtypeStruct((batch * t, W_DSA), BF16),
        scratch_shapes=[pltpu.VMEM((row, tq), I32),
                        pltpu.VMEM((row, tq), I16),
                        pltpu.VMEM((H_IDX, D_IDX, tq), BF16),
                        pltpu.VMEM((H_DSA // 2, LANES, 2 * tq), BF16),
                        pltpu.VMEM((H_IDX, tq), F32),
                        pltpu.VMEM((H_DSA // 2, 1, 2 * tq), F32),
                        pltpu.VMEM((H_DSA // 2, LANES + DEN_ROWS, 2 * tq), F32),
                        pltpu.VMEM((H_DSA // 2, 1, 2 * tq), F32)],
        compiler_params=pltpu.CompilerParams(dimension_semantics=("arbitrary", "arbitrary"),
                                             vmem_limit_bytes=VMEM_LIMIT_BYTES),
        name="dsa",
    )(*operands)


def _ffn_kernel(x_ref, yr_ref, od_ref, wo_ref, g2_ref, wup_ref, wg_ref, wd_ref, cw_ref, cb_ref, cs_ref, gf_ref,
                y_ref, cnew_ref, carry_scr, *, fc):
    t = pl.program_id(1)
    tm = x_ref.shape[0]
    d_ff = wup_ref.shape[1]

    @pl.when(t == 0)
    def _():
        carry_scr[...] = cs_ref[0]

    x1 = x_ref[...] + _dot(yr_ref[...], wo_ref[:W_RET, :]) + _dot(od_ref[...], wo_ref[W_RET:, :])
    h2 = (_rms(x1) * g2_ref[...]).astype(BF16)
    row = lax.broadcasted_iota(I32, (tm, fc), 0)
    acc = jnp.zeros(x1.shape, F32)
    for c0 in range(0, d_ff, fc):
        cols = slice(c0, c0 + fc)
        a = _dot(h2, wup_ref[:, cols])
        u = _dot(h2, wg_ref[:, cols])
        prev2 = carry_scr[0:1, cols]
        prev1 = carry_scr[1:2, cols]
        a1 = jnp.where(row == 0, prev1, pltpu.roll(a, 1, 0))
        a2 = jnp.where(row == 0, prev2, jnp.where(row == 1, prev1, pltpu.roll(a, 2, 0)))
        cw = cw_ref[:, cols]
        conv = cb_ref[:, cols] + a2 * cw[0:1] + a1 * cw[1:2] + a * cw[2:3]
        carry_scr[:, cols] = a[tm - 2:tm, :]
        acc = acc + _dot((_silu(conv) * u).astype(BF16), wd_ref[cols, :])
    y_ref[...] = _rms(x1 + acc) * gf_ref[...]

    @pl.when(t == pl.num_programs(1) - 1)
    def _():
        cnew_ref[0] = carry_scr[...]


def _out_ffn(x2d, y_ret, o_dsa, w_out, g2, w_up, w_gate, w_down, conv_w, conv_b, conv_state, g_final,
             batch, t, tm):
    n, d = x2d.shape
    d_ff = w_up.shape[1]
    nt = t // tm
    fc = 256
    assert d_ff % fc == 0
    row = lambda b, i: (b * nt + i, 0)
    const = lambda b, i: (0, 0)
    per_b = lambda b, i: (b, 0, 0)
    return pl.pallas_call(
        functools.partial(_ffn_kernel, fc=fc),
        grid=(batch, nt),
        in_specs=[pl.BlockSpec((tm, d), row), pl.BlockSpec((tm, W_RET), row), pl.BlockSpec((tm, W_DSA), row),
                  pl.BlockSpec(w_out.shape, const), pl.BlockSpec((1, d), const),
                  pl.BlockSpec(w_up.shape, const), pl.BlockSpec(w_gate.shape, const),
                  pl.BlockSpec(w_down.shape, const), pl.BlockSpec(conv_w.shape, const),
                  pl.BlockSpec((1, d_ff), const), pl.BlockSpec((1, CONV_W - 1, d_ff), per_b),
                  pl.BlockSpec((1, d), const)],
        out_specs=(pl.BlockSpec((tm, d), row), pl.BlockSpec((1, CONV_W - 1, d_ff), per_b)),
        out_shape=(jax.ShapeDtypeStruct((n, d), F32), jax.ShapeDtypeStruct((batch, CONV_W - 1, d_ff), F32)),
        scratch_shapes=[pltpu.VMEM((CONV_W - 1, d_ff), F32)],
        compiler_params=pltpu.CompilerParams(dimension_semantics=("arbitrary", "arbitrary"),
                                             vmem_limit_bytes=VMEM_LIMIT_BYTES),
        name="out_ffn",
    )(x2d, y_ret, o_dsa, w_out, g2, w_up, w_gate, w_down, conv_w, conv_b, conv_state, g_final)


def _rope_tables(past, t, rows):
    half = DK_RET // 2
    inv_freq = ROPE_BASE ** (-jnp.arange(half, dtype=F32) / half)
    pos = past + jnp.arange(t, dtype=jnp.int32)
    ang = pos.astype(F32)[:, None] * inv_freq[None, :]
    cos, sin = jnp.cos(ang), jnp.sin(ang)
    cos_t = jnp.concatenate([cos, cos, cos, cos], axis=1)
    sin_t = jnp.concatenate([-sin, sin, -sin, sin], axis=1)
    if t < rows:
        cos_t = jnp.tile(cos_t, (rows // t, 1))
        sin_t = jnp.tile(sin_t, (rows // t, 1))
    return cos_t, sin_t


def _trunk_layer(x, ret_state, past_kv, conv_state, weights, g_final):
    w_in_pad, w_out, g_mix, g_gn, g_ffn, w_up, w_gate, w_down, conv_w, conv_b = weights
    batch, t, d = x.shape
    n = batch * t
    past = 0 if past_kv is None else past_kv[0].shape[1]
    x2d = x.reshape(n, d)

    tm_in = min(512, n)
    cos_t, sin_t = _rope_tables(past, t, tm_in)
    (rq, rk, rv, rg, dq, dk, dv, dkb, dvb, iq, ik, ikb, iw) = _in_proj(x2d, g_mix, w_in_pad, cos_t, sin_t, tm_in)

    y_ret, ret_new = _retention(rq, rk, rv, rg, ret_state, g_gn, batch, t)

    if past_kv is None:
        key_sets = [(t, ikb.reshape(batch, t, D_IDX), dkb.reshape(batch, t, W_DSA), dvb.reshape(batch, t, W_DSA))]
    else:
        past_k, past_v, past_ki = past_kv
        rows = -(-t // LANES) * LANES
        padded = lambda a, w: jnp.pad(a.reshape(batch, t, w), ((0, 0), (0, rows - t), (0, 0)))
        key_sets = [(past, past_ki, past_k.reshape(batch, past, W_DSA), past_v.reshape(batch, past, W_DSA)),
                    (t, padded(ikb, D_IDX), padded(dkb, W_DSA), padded(dvb, W_DSA))]
    tq = min(256, -(-t // LANES) * LANES)
    if t % tq:
        qpad = lambda a: jnp.pad(a.reshape(batch, t, -1), ((0, 0), (0, tq - t), (0, 0))).reshape(batch * tq, -1)
        o_dsa = _dsa(qpad(iq), qpad(dq), qpad(iw), key_sets, batch, tq, past, tq)
        o_dsa = o_dsa.reshape(batch, tq, W_DSA)[:, :t].reshape(n, W_DSA)
    else:
        o_dsa = _dsa(iq, dq, iw, key_sets, batch, t, past, tq)

    y, conv_new = _out_ffn(x2d, y_ret, o_dsa, w_out, g_ffn, w_up, w_gate, w_down, conv_w, conv_b,
                           conv_state, g_final, batch, t, min(512, t))
    return (y.reshape(batch, t, d), dk.reshape(batch, t, H_DSA, DH_DSA), dv.reshape(batch, t, H_DSA, DH_DSA),
            ik.reshape(batch, t, D_IDX), ret_new, conv_new)


def kernel(x_prompt, x_sample, cache_dsa_k, cache_dsa_v, cache_idx_k, state_ret, state_ffn_conv, w_in, w_out,
           g_norm_mix, g_gn_ret, g_norm_ffn, w_up, w_gate, w_down, conv_w, conv_b, g_norm_final):
    depth = w_in.shape[0]
    assert depth == 1, "the final norm is fused into the layer's last kernel"
    bp = x_prompt.shape[0]
    d_ff = w_up.shape[-1]
    l = 0
    p_in = w_in.shape[-1]
    pad = (-p_in) % LANES
    weights = (jnp.pad(w_in[l], ((0, 0), (0, pad))).astype(BF16), w_out[l].astype(BF16),
               g_norm_mix[l][None, :], g_gn_ret[l][None, :], g_norm_ffn[l][None, :],
               w_up[l].astype(BF16), w_gate[l].astype(BF16), w_down[l].astype(BF16),
               conv_w[l], conv_b[l][None, :])
    g_final = g_norm_final[None, :]

    yp, pk, pv, pki, pret, pconv = _trunk_layer(
        x_prompt, jnp.zeros((bp, H_RET, DK_RET, DV_RET), F32), None,
        jnp.zeros((bp, CONV_W - 1, d_ff), F32), weights, g_final)
    ys, sk, sv, ski, sret, sconv = _trunk_layer(
        x_sample, state_ret[l], (cache_dsa_k[l], cache_dsa_v[l], cache_idx_k[l]),
        state_ffn_conv[l], weights, g_final)
    st = lambda a: a[None]
    return (yp, ys, st(pk), st(pv), st(pki), st(pret), st(pconv),
            st(sk), st(sv), st(ski), st(sret), st(sconv))
```

```python
import functools

import numpy as np
import jax
import jax.numpy as jnp
from jax import lax
from jax.experimental import pallas as pl
from jax.experimental.pallas import tpu as pltpu

F32 = jnp.float32
BF16 = jnp.bfloat16
I32 = jnp.int32
I16 = jnp.int16
COARSE = jnp.bfloat16

CHUNK = 64
H_RET = 8
DK_RET = 64
DV_RET = 64
H_DSA = 8
DH_DSA = 64
H_IDX = 8
D_IDX = 64
TOPK_MAX = 256
CONV_W = 3
ROPE_BASE = 10000.0
EPS = 1e-6
W_RET = H_RET * DV_RET
W_DSA = H_DSA * DH_DSA

LANES = 128
SUBLANES = 8
HALF_ROWS = 16
DEN_ROWS = 16
VMEM_LIMIT_BYTES = 56 * 1024 * 1024
INT_MIN = -(2 ** 31)
F32_LOWEST = -3.4028234663852886e38
NEG_BIG = -1e30
LOG2E = 1.4426950408889634
SAFE_DENOM_MIN = 2.0 ** -100
SAFE_DENOM_MAX = 2.0 ** 100


def _dot(a, b):
    return jnp.dot(a, b, preferred_element_type=F32)


def _dot_nt(a, b):
    return lax.dot_general(a, b, (((1,), (1,)), ((), ())), preferred_element_type=F32)


def _dot_tn(a, b):
    return lax.dot_general(a, b, (((0,), (0,)), ((), ())), preferred_element_type=F32)


def _rms(x):
    return x * lax.rsqrt(jnp.mean(x * x, axis=-1, keepdims=True) + EPS)


def _silu(x):
    return x / (1.0 + jnp.exp(-x))


def _inproj_kernel(x_ref, g_ref, w_ref, cos_ref, sin_ref,
                   rq_ref, rk_ref, rv_ref, rg_ref, dq_ref, dk_ref, dv_ref, dkb_ref, dvb_ref,
                   iq_ref, ik_ref, ikb_ref, iw_ref):
    tm = x_ref.shape[0]
    h = (_rms(x_ref[...]) * g_ref[...]).astype(BF16)

    def proj(c0, width):
        return _dot(h, w_ref[:, c0:c0 + width])

    cos = jnp.concatenate([cos_ref[...]] * (W_RET // LANES), axis=1)
    sin = jnp.concatenate([sin_ref[...]] * (W_RET // LANES), axis=1)
    lane = lax.broadcasted_iota(I32, (tm, W_RET), 1)
    first_half = (lane & (DK_RET - 1)) < (DK_RET // 2)

    def rope(p):
        swapped = jnp.where(first_half, pltpu.roll(p, W_RET - DK_RET // 2, 1), pltpu.roll(p, DK_RET // 2, 1))
        return p * cos + swapped * sin

    rq_ref[...] = rope(proj(0, 512)).astype(BF16)
    rk_ref[...] = rope(proj(512, 512)) * (DK_RET ** -0.5)
    rv_ref[...] = proj(1024, 512).astype(BF16)
    rg_ref[...] = proj(1536, 512)
    dq_ref[...] = (proj(2048, 512) * (DH_DSA ** -0.5 * LOG2E)).astype(BF16)
    dk = proj(2560, 512)
    dk_ref[...] = dk
    dkb_ref[...] = dk.astype(BF16)
    dv = proj(3072, 512)
    dv_ref[...] = dv
    dvb_ref[...] = dv.astype(BF16)
    iq_ref[...] = (proj(3584, 512) * (D_IDX ** -0.5)).astype(BF16)
    tail = proj(4096, LANES)
    ik = tail[:, :D_IDX]
    ik_ref[...] = ik
    ikb_ref[...] = ik.astype(BF16)
    iw_ref[...] = tail[:, D_IDX:D_IDX + H_IDX] * (H_IDX ** -0.5)


def _in_proj(x2d, g, w_pad, cos_t, sin_t, tm):
    n, d = x2d.shape
    nt = n // tm
    ntab = cos_t.shape[0] // tm
    row = lambda i: (i, 0)
    const = lambda i: (0, 0)
    tab = lambda i: (i % ntab, 0)
    wide = lambda dt: jax.ShapeDtypeStruct((n, 512), dt)
    out_shape = (wide(BF16), wide(F32), wide(BF16), wide(F32), wide(BF16), wide(F32), wide(F32),
                 wide(BF16), wide(BF16), wide(BF16),
                 jax.ShapeDtypeStruct((n, D_IDX), F32), jax.ShapeDtypeStruct((n, D_IDX), BF16),
                 jax.ShapeDtypeStruct((n, H_IDX), F32))
    out_specs = tuple([pl.BlockSpec((tm, 512), row)] * 10
                      + [pl.BlockSpec((tm, D_IDX), row), pl.BlockSpec((tm, D_IDX), row),
                         pl.BlockSpec((tm, H_IDX), row)])
    return pl.pallas_call(
        _inproj_kernel,
        grid=(nt,),
        in_specs=[pl.BlockSpec((tm, d), row), pl.BlockSpec((1, d), const),
                  pl.BlockSpec(w_pad.shape, const),
                  pl.BlockSpec((tm, LANES), tab), pl.BlockSpec((tm, LANES), tab)],
        out_specs=out_specs,
        out_shape=out_shape,
        compiler_params=pltpu.CompilerParams(dimension_semantics=("arbitrary",),
                                             vmem_limit_bytes=VMEM_LIMIT_BYTES),
        name="in_proj",
    )(x2d, g, w_pad, cos_t, sin_t)


def _ret_kernel(q_ref, k_ref, v_ref, g_ref, s0_ref, dec_ref, xi_ref, zeta_ref, gn_ref,
                y_ref, sout_ref, s_scr, *, gamma_c):
    c = pl.program_id(1)

    @pl.when(c == 0)
    def _():
        s_scr[...] = s0_ref[0]

    q = q_ref[...]
    k = k_ref[...]
    v = v_ref[...]
    kb = k.astype(BF16)
    kz = (k * zeta_ref[...]).astype(BF16)
    xi = xi_ref[...]
    normed = []
    for h in range(H_RET):
        sl = slice(DK_RET * h, DK_RET * (h + 1))
        qh, kh, vh, kzh = q[:, sl], kb[:, sl], v[:, sl], kz[:, sl]
        state = s_scr[h]
        scores = _dot_nt(qh, kh) * dec_ref[h]
        inner = _dot(scores.astype(BF16), vh)
        cross = _dot(qh, state.astype(BF16)) * xi[:, sl]
        o = inner + cross
        s_scr[h] = gamma_c[h] * state + _dot_tn(kzh, vh)
        mu = jnp.mean(o, axis=-1, keepdims=True)
        dev = o - mu
        var = jnp.mean(dev * dev, axis=-1, keepdims=True)
        normed.append(dev * lax.rsqrt(var + EPS))
    on = jnp.concatenate(normed, axis=1) * gn_ref[...]
    y_ref[...] = (_silu(g_ref[...]) * on).astype(BF16)

    @pl.when(c == pl.num_programs(1) - 1)
    def _():
        sout_ref[0] = s_scr[...]


def _retention_constants(c):
    lg = np.log1p(-np.exp2(-5.0 - np.arange(H_RET, dtype=np.float64)))
    idx = np.arange(c, dtype=np.float64)
    diff = idx[:, None] - idx[None, :]
    decay = np.where(diff[None] >= 0, np.exp(lg[:, None, None] * np.maximum(diff, 0.0)[None]), 0.0)
    xi = np.exp(lg[None, :] * (idx + 1.0)[:, None])
    zeta = np.exp(lg[None, :] * (c - 1.0 - idx)[:, None])
    gamma_c = tuple(float(np.float32(g)) for g in np.exp(lg * c))
    rep = lambda a: np.repeat(a, DK_RET, axis=1).astype(np.float32)
    return decay.astype(np.float32), rep(xi), rep(zeta), gamma_c


def _retention(rq, rk, rv, rg, s0, gn, batch, t):
    c = min(2 * CHUNK, t)
    nc = t // c
    decay, xi, zeta, gamma_c = _retention_constants(c)
    blk = lambda b, j: (b * nc + j, 0)
    const2 = lambda b, j: (0, 0)
    const3 = lambda b, j: (0, 0, 0)
    st = lambda b, j: (b, 0, 0, 0)
    n = batch * t
    return pl.pallas_call(
        functools.partial(_ret_kernel, gamma_c=gamma_c),
        grid=(batch, nc),
        in_specs=[pl.BlockSpec((c, W_RET), blk)] * 4
                 + [pl.BlockSpec((1, H_RET, DK_RET, DV_RET), st),
                    pl.BlockSpec((H_RET, c, c), const3),
                    pl.BlockSpec((c, W_RET), const2), pl.BlockSpec((c, W_RET), const2),
                    pl.BlockSpec((1, W_RET), const2)],
        out_specs=(pl.BlockSpec((c, W_RET), blk), pl.BlockSpec((1, H_RET, DK_RET, DV_RET), st)),
        out_shape=(jax.ShapeDtypeStruct((n, W_RET), BF16),
                   jax.ShapeDtypeStruct((batch, H_RET, DK_RET, DV_RET), F32)),
        scratch_shapes=[pltpu.VMEM((H_RET, DK_RET, DV_RET), F32)],
        compiler_params=pltpu.CompilerParams(dimension_semantics=("arbitrary", "arbitrary"),
                                             vmem_limit_bytes=VMEM_LIMIT_BYTES),
        name="retention",
    )(rq, rk, rv, rg, s0, jnp.asarray(decay), jnp.asarray(xi), jnp.asarray(zeta), gn)


def _log2(n):
    assert n > 0 and n & (n - 1) == 0
    return n.bit_length() - 1


def _dsa_kernel(*refs, segs, tq, topk, q_pos0):
    nseg = len(segs)
    iq_ref, dq_ref, iw_ref = refs[:3]
    seg_refs = [refs[3 + 3 * s: 6 + 3 * s] for s in range(nseg)]
    o_ref = refs[3 + 3 * nseg]
    sc_scr, half_scr, qi_scr, qm_scr, w_scr, m_scr, acc_scr, kmax_scr = refs[4 + 3 * nseg:]

    chunk_shift = _log2(CHUNK)
    qb = pl.program_id(1)
    q_first = q_pos0 + qb * tq
    q_last_chunk = lax.shift_right_logical(q_first + tq - 1, chunk_shift)
    q_chunk = lax.shift_right_logical(q_first + lax.broadcasted_iota(I32, (1, tq), 1), chunk_shift)

    iq_t = iq_ref[...].astype(F32).T
    dq_t = dq_ref[...].astype(F32).T
    w_scr[...] = iw_ref[...].T
    pair_row = lax.broadcasted_iota(I32, (LANES, tq), 0)
    for h in range(H_IDX):
        qi_scr[h] = iq_t[D_IDX * h:D_IDX * (h + 1)].astype(BF16)
    for hp in range(H_DSA // 2):
        pair = dq_t[LANES * hp:LANES * (hp + 1)]
        qm_scr[hp] = jnp.concatenate([jnp.where(pair_row < DH_DSA, pair, 0.0),
                                      jnp.where(pair_row >= DH_DSA, pair, 0.0)], axis=1).astype(BF16)
    acc_scr[...] = jnp.zeros(acc_scr.shape, F32)

    def blocks_needed(seg):
        valid, blk, _, pos0 = seg
        adm = jnp.clip((q_last_chunk + 1) * CHUNK - pos0, 0, valid)
        return lax.shift_right_logical(adm + blk - 1, _log2(blk))

    def rows_of(seg, j):
        _, blk, row0, _ = seg
        return pl.ds(pl.multiple_of(row0 + j * blk, blk), blk)

    for seg, (kidx_ref, _, _) in zip(segs, seg_refs):
        valid, blk, _, pos0 = seg

        def score_block(j, carry, seg=seg, kidx_ref=kidx_ref, valid=valid, blk=blk, pos0=pos0):
            kidx = kidx_ref[0, pl.ds(pl.multiple_of(j * blk, blk), blk), :].astype(BF16)
            tot = jnp.zeros((blk, tq), F32)
            for h in range(H_IDX):
                tot = tot + jnp.maximum(_dot(kidx, qi_scr[h]), 0.0) * w_scr[h:h + 1, :]
            idx = j * blk + lax.broadcasted_iota(I32, (blk, tq), 0)
            ok = (lax.shift_right_logical(pos0 + idx, chunk_shift) <= q_chunk) & (idx < valid)
            score = jnp.where(ok, tot, -jnp.inf)
            sc_scr[rows_of(seg, j), :] = score
            near = score.astype(COARSE)
            bits = pltpu.bitcast(near, I16)
            below = pltpu.bitcast(bits + jnp.where(bits < 0, jnp.int16(1), jnp.int16(-1)), COARSE)
            half_scr[rows_of(seg, j), :] = jnp.where(near.astype(F32) > score, below, near)
            return carry

        lax.fori_loop(0, blocks_needed(seg), score_block, 0)

    def fold_rows(a, rows):
        parts = [a[rows * r:rows * (r + 1)] for r in range(a.shape[0] // rows)]
        while len(parts) > 1:
            parts = [parts[i] + parts[i + 1] for i in range(0, len(parts) - 1, 2)] + parts[len(parts) & ~1:]
        return parts[0]

    def count_rows(scr, rows, one, hit_fn):
        cnt = jnp.zeros((rows, tq), one.dtype)
        for seg in segs:
            blk, row0 = seg[1], seg[2]

            def body(j, cnt, seg=seg, blk=blk, row0=row0):
                v = scr[rows_of(seg, j), :]
                sub = 16 * rows
                for c in range(0, blk, sub):
                    hit = hit_fn(v[c:c + sub], row0 + j * blk + c)
                    cnt = cnt + fold_rows(jnp.where(hit, one, jnp.zeros_like(one)), rows)
                return cnt

            cnt = lax.fori_loop(0, blocks_needed(seg), body, cnt)
        return jnp.sum(cnt.astype(I32), axis=0, keepdims=True)

    def key_to_f32(key):
        return pltpu.bitcast(jnp.where(key < 0, key ^ jnp.int32(0x7FFFFFFF), key), F32)

    def coarse_bits(key16):
        return jnp.where(key16 < 0, key16 ^ jnp.int32(0x7FFF), key16)

    def largest_passing(n_all, count_ge):
        def step(i, carry):
            ans, n_ans = carry
            cand = jnp.where(i == 0, jnp.zeros_like(ans), ans | jnp.left_shift(jnp.int32(1), 15 - i))
            n = count_ge(cand)
            take = n >= topk
            return jnp.where(take, cand, ans), jnp.where(take, n, n_ans)

        return lax.fori_loop(0, 16, step, (jnp.full((1, tq), -32768, I32), n_all))

    total = jnp.zeros((1, tq), I32)
    for seg in segs:
        total = total + blocks_needed(seg) * seg[1]

    one16 = jnp.ones((1, 1), I16)
    one32 = jnp.ones((1, 1), I32)

    def count_coarse_ge(key16):
        cand = pltpu.bitcast(coarse_bits(key16).astype(I16), COARSE)
        return count_rows(half_scr, HALF_ROWS, one16, lambda v, r0: v >= cand)

    key16, n_coarse = largest_passing(total, count_coarse_ge)
    short = key16 == -32768
    base = lax.shift_left(coarse_bits(key16), 16)
    base = jnp.where(base < 0, base ^ jnp.int32(0x7FFFFFFF), base)

    def count_fine_ge(off):
        cand = key_to_f32(base + off + 32768)
        return count_rows(sc_scr, SUBLANES, one32, lambda v, r0: v >= cand)

    off, n_ge = largest_passing(n_coarse, count_fine_ge)
    thr = jnp.where(short, -jnp.inf, key_to_f32(jnp.where(short, 0, base + off + 32768)))

    need = (n_ge > topk) & jnp.logical_not(short)

    @pl.when(jnp.max(jnp.where(need, 1, 0)) > 0)
    def _():
        def count_scores(hit_fn):
            def with_rows(v, r0):
                return hit_fn(v, r0 + lax.broadcasted_iota(I32, v.shape, 0))

            return count_rows(sc_scr, SUBLANES, one32, with_rows)

        quota = topk - count_scores(lambda v, ridx: v > thr)
        nbits = int(sc_scr.shape[0]).bit_length()

        def index_bisect(i, last):
            cand = last | jnp.left_shift(jnp.int32(1), nbits - 1 - i)
            below = count_scores(lambda v, ridx: (v == thr) & (ridx < cand))
            return jnp.where(below < quota, cand, last)

        last = lax.fori_loop(0, nbits, index_bisect, jnp.zeros((1, tq), I32))

        for seg in segs:
            blk, row0 = seg[1], seg[2]

            def drop(j, carry, seg=seg, blk=blk, row0=row0):
                v = sc_scr[rows_of(seg, j), :]
                ridx = row0 + j * blk + lax.broadcasted_iota(I32, (blk, tq), 0)
                cut = need & (v == thr) & (ridx > last)
                sc_scr[rows_of(seg, j), :] = jnp.where(cut, -jnp.inf, v)
                return carry

            lax.fori_loop(0, blocks_needed(seg), drop, 0)

    thr_sel = jnp.maximum(thr, F32_LOWEST)

    def attend_all(body):
        for seg, (_, k_ref, v_ref) in zip(segs, seg_refs):
            blk = seg[1]

            def step(j, carry, seg=seg, k_ref=k_ref, v_ref=v_ref, blk=blk):
                bias = jnp.where(sc_scr[rows_of(seg, j), :] >= thr_sel, 0.0, NEG_BIG)
                bias = jnp.concatenate([bias, bias], axis=1)
                rows = pl.ds(pl.multiple_of(j * blk, blk), blk)
                kblk = k_ref[0, rows, :].astype(BF16)
                vblk = v_ref[0, rows, :].astype(F32)
                ones = jnp.ones((DEN_ROWS, blk), BF16)
                for hp in range(H_DSA // 2):
                    lanes = slice(LANES * hp, LANES * (hp + 1))
                    s = _dot(kblk[:, lanes], qm_scr[hp]) + bias
                    v_aug = jnp.concatenate([vblk[:, lanes].T.astype(BF16), ones], axis=0)
                    body(hp, s, v_aug)
                return carry

            lax.fori_loop(0, blocks_needed(seg), step, 0)

    @pl.when(qb == 0)
    def _():
        feat = lax.shift_right_logical(lax.broadcasted_iota(I32, (W_DSA, LANES), 0), _log2(DH_DSA))
        head_sum = jnp.where(feat == lax.broadcasted_iota(I32, (W_DSA, LANES), 1), 1.0, 0.0).astype(BF16)
        best = jnp.zeros((1, LANES), F32)
        for seg, (_, k_ref, _) in zip(segs, seg_refs):
            blk = seg[1]

            def norms(j, best, k_ref=k_ref, blk=blk):
                kb = k_ref[0, pl.ds(pl.multiple_of(j * blk, blk), blk), :].astype(F32)
                sq = _dot((kb * kb).astype(BF16), head_sum)
                return jnp.maximum(best, jnp.max(sq, axis=0, keepdims=True))

            best = lax.fori_loop(0, k_ref.shape[1] // blk, norms, best)
        knorm = jnp.sqrt(best)
        for h in range(H_DSA):
            kmax_scr[h // 2, :, tq * (h % 2):tq * (h % 2 + 1)] = jnp.broadcast_to(knorm[:, h:h + 1], (1, tq))

    for hp in range(H_DSA // 2):
        qp = qm_scr[hp].astype(F32)
        m_scr[hp] = jnp.sqrt(jnp.sum(qp * qp, axis=0, keepdims=True)) * kmax_scr[hp]

    def fast_pair(hp, s, v_aug):
        acc_scr[hp] = acc_scr[hp] + _dot(v_aug, jnp.exp2(s - m_scr[hp]).astype(BF16))

    attend_all(fast_pair)

    bad = jnp.zeros((1, 2 * tq), I32)
    for hp in range(H_DSA // 2):
        den = acc_scr[hp][LANES:LANES + 1]
        bad = bad | jnp.where((den >= SAFE_DENOM_MIN) & (den <= SAFE_DENOM_MAX), 0, 1)

    @pl.when(jnp.max(bad) > 0)
    def _():
        m_scr[...] = jnp.full(m_scr.shape, NEG_BIG, F32)
        acc_scr[...] = jnp.zeros(acc_scr.shape, F32)

        def online_pair(hp, s, v_aug):
            m_old = m_scr[hp]
            m_new = jnp.maximum(m_old, jnp.max(s, axis=0, keepdims=True))
            acc_scr[hp] = (jnp.exp2(m_old - m_new) * acc_scr[hp]
                           + _dot(v_aug, jnp.exp2(s - m_new).astype(BF16)))
            m_scr[hp] = m_new

        attend_all(online_pair)

    for hp in range(H_DSA // 2):
        acc = acc_scr[hp]
        den = acc[LANES:LANES + 1]
        pair_t = jnp.concatenate([acc[:DH_DSA, :tq] / den[:, :tq], acc[DH_DSA:LANES, tq:] / den[:, tq:]], axis=0)
        o_ref[:, LANES * hp:LANES * (hp + 1)] = pair_t.T.astype(BF16)


def _dsa(iq, dq, iw, key_sets, batch, t, past, tq):
    nq = t // tq
    ltot = sum(ks[0] for ks in key_sets)
    topk = min(TOPK_MAX, ltot // 4)
    segs = []
    row = 0
    pos = 0
    for valid, kidx, _, _ in key_sets:
        length = kidx.shape[1]
        blk = min(512, length)
        assert blk % LANES == 0 and length % blk == 0 and valid <= length
        segs.append((valid, blk, row, pos))
        row += length
        pos += valid
    assert topk <= row and tq % LANES == 0
    qrow = lambda b, i: (b * nq + i, 0)
    per_b = lambda b, i: (b, 0, 0)
    in_specs = [pl.BlockSpec((tq, 512), qrow), pl.BlockSpec((tq, 512), qrow), pl.BlockSpec((tq, H_IDX), qrow)]
    operands = [iq, dq, iw]
    for _, kidx, k, v in key_sets:
        length = kidx.shape[1]
        in_specs += [pl.BlockSpec((1, length, D_IDX), per_b), pl.BlockSpec((1, length, W_DSA), per_b),
                     pl.BlockSpec((1, length, W_DSA), per_b)]
        operands += [kidx, k, v]
    return pl.pallas_call(
        functools.partial(_dsa_kernel, segs=tuple(segs), tq=tq, topk=topk, q_pos0=past),
        grid=(batch, nq),
        in_specs=in_specs,
        out_specs=pl.BlockSpec((tq, W_DSA), qrow),
        out_shape=jax.ShapeDtypeStruct((batch * t, W_DSA), BF16),
        scratch_shapes=[pltpu.VMEM((row, tq), F32),
                        pltpu.VMEM((row, tq), COARSE),
                        pltpu.VMEM((H_IDX, D_IDX, tq), BF16),
                        pltpu.VMEM((H_DSA // 2, LANES, 2 * tq), BF16),
                        pltpu.VMEM((H_IDX, tq), F32),
                        pltpu.VMEM((H_DSA // 2, 1, 2 * tq), F32),
                        pltpu.VMEM((H_DSA // 2, LANES + DEN_ROWS, 2 * tq), F32),
                        pltpu.VMEM((H_DSA // 2, 1, 2 * tq), F32)],
        compiler_params=pltpu.CompilerParams(dimension_semantics=("arbitrary", "arbitrary"),
                                             vmem_limit_bytes=VMEM_LIMIT_BYTES),
        name="dsa",
    )(*operands)


def _ffn_kernel(x_ref, yr_ref, od_ref, wo_ref, g2_ref, wup_ref, wg_ref, wd_ref, cw_ref, cb_ref, cs_ref, gf_ref,
                y_ref, cnew_ref, carry_scr, *, fc):
    t = pl.program_id(1)
    tm = x_ref.shape[0]
    d_ff = wup_ref.shape[1]
    nseq = carry_scr.shape[0]
    seq = tm // nseq

    @pl.when(t == 0)
    def _():
        carry_scr[...] = cs_ref[...]

    x1 = x_ref[...] + _dot(yr_ref[...], wo_ref[:W_RET, :]) + _dot(od_ref[...], wo_ref[W_RET:, :])
    h2 = (_rms(x1) * g2_ref[...]).astype(BF16)
    row = lax.broadcasted_iota(I32, (tm, fc), 0) & (seq - 1)
    acc = jnp.zeros(x1.shape, F32)

    def carried(r, cols):
        if nseq == 1:
            return carry_scr[0, r:r + 1, cols]
        return jnp.concatenate([jnp.broadcast_to(carry_scr[s, r:r + 1, cols], (seq, fc)) for s in range(nseq)], axis=0)

    for c0 in range(0, d_ff, fc):
        cols = slice(c0, c0 + fc)
        a = _dot(h2, wup_ref[:, cols])
        u = _dot(h2, wg_ref[:, cols])
        prev2, prev1 = carried(0, cols), carried(1, cols)
        a1 = jnp.where(row == 0, prev1, pltpu.roll(a, 1, 0))
        a2 = jnp.where(row == 0, prev2, jnp.where(row == 1, prev1, pltpu.roll(a, 2, 0)))
        cw = cw_ref[:, cols]
        conv = cb_ref[:, cols] + a2 * cw[0:1] + a1 * cw[1:2] + a * cw[2:3]
        for s in range(nseq):
            carry_scr[s, :, cols] = a[(s + 1) * seq - 2:(s + 1) * seq, :]
        acc = acc + _dot((_silu(conv) * u).astype(BF16), wd_ref[cols, :])
    y_ref[...] = _rms(x1 + acc) * gf_ref[...]

    @pl.when(t == pl.num_programs(1) - 1)
    def _():
        cnew_ref[...] = carry_scr[...]


def _out_ffn(x2d, y_ret, o_dsa, w_out, g2, w_up, w_gate, w_down, conv_w, conv_b, conv_state, g_final,
             batch, t, tm):
    n, d = x2d.shape
    d_ff = w_up.shape[1]
    nseq = max(1, tm // t)
    assert tm % nseq == 0 and (tm // nseq) & (tm // nseq - 1) == 0 and batch % nseq == 0 and t % (tm // nseq) == 0
    nt = t * nseq // tm
    fc = 256
    assert d_ff % fc == 0
    row = lambda g, i: (g * nt + i, 0)
    const = lambda g, i: (0, 0)
    per_g = lambda g, i: (g, 0, 0)
    return pl.pallas_call(
        functools.partial(_ffn_kernel, fc=fc),
        grid=(batch // nseq, nt),
        in_specs=[pl.BlockSpec((tm, d), row), pl.BlockSpec((tm, W_RET), row), pl.BlockSpec((tm, W_DSA), row),
                  pl.BlockSpec(w_out.shape, const), pl.BlockSpec((1, d), const),
                  pl.BlockSpec(w_up.shape, const), pl.BlockSpec(w_gate.shape, const),
                  pl.BlockSpec(w_down.shape, const), pl.BlockSpec(conv_w.shape, const),
                  pl.BlockSpec((1, d_ff), const), pl.BlockSpec((nseq, CONV_W - 1, d_ff), per_g),
                  pl.BlockSpec((1, d), const)],
        out_specs=(pl.BlockSpec((tm, d), row), pl.BlockSpec((nseq, CONV_W - 1, d_ff), per_g)),
        out_shape=(jax.ShapeDtypeStruct((n, d), F32), jax.ShapeDtypeStruct((batch, CONV_W - 1, d_ff), F32)),
        scratch_shapes=[pltpu.VMEM((nseq, CONV_W - 1, d_ff), F32)],
        compiler_params=pltpu.CompilerParams(dimension_semantics=("arbitrary", "arbitrary"),
                                             vmem_limit_bytes=VMEM_LIMIT_BYTES),
        name="out_ffn",
    )(x2d, y_ret, o_dsa, w_out, g2, w_up, w_gate, w_down, conv_w, conv_b, conv_state, g_final)


def _rope_tables(past, t, rows):
    half = DK_RET // 2
    inv_freq = ROPE_BASE ** (-jnp.arange(half, dtype=F32) / half)
    pos = past + jnp.arange(t, dtype=jnp.int32)
    ang = pos.astype(F32)[:, None] * inv_freq[None, :]
    cos, sin = jnp.cos(ang), jnp.sin(ang)
    cos_t = jnp.concatenate([cos, cos, cos, cos], axis=1)
    sin_t = jnp.concatenate([-sin, sin, -sin, sin], axis=1)
    if t < rows:
        cos_t = jnp.tile(cos_t, (rows // t, 1))
        sin_t = jnp.tile(sin_t, (rows // t, 1))
    return cos_t, sin_t


def _trunk_layer(x, ret_state, past_kv, conv_state, weights, g_final):
    w_in_pad, w_out, g_mix, g_gn, g_ffn, w_up, w_gate, w_down, conv_w, conv_b = weights
    batch, t, d = x.shape
    n = batch * t
    past = 0 if past_kv is None else past_kv[0].shape[1]
    x2d = x.reshape(n, d)

    tm_in = min(512, n)
    cos_t, sin_t = _rope_tables(past, t, tm_in)
    (rq, rk, rv, rg, dq, dk, dv, dkb, dvb, iq, ik, ikb, iw) = _in_proj(x2d, g_mix, w_in_pad, cos_t, sin_t, tm_in)

    y_ret, ret_new = _retention(rq, rk, rv, rg, ret_state, g_gn, batch, t)

    if past_kv is None:
        key_sets = [(t, ikb.reshape(batch, t, D_IDX), dkb.reshape(batch, t, W_DSA), dvb.reshape(batch, t, W_DSA))]
    else:
        past_k, past_v, past_ki = past_kv
        rows = -(-t // LANES) * LANES
        padded = lambda a, w: jnp.pad(a.reshape(batch, t, w), ((0, 0), (0, rows - t), (0, 0)))
        key_sets = [(past, past_ki, past_k.reshape(batch, past, W_DSA), past_v.reshape(batch, past, W_DSA)),
                    (t, padded(ikb, D_IDX), padded(dkb, W_DSA), padded(dvb, W_DSA))]
    tq = min(256, -(-t // LANES) * LANES)
    if t % tq:
        qpad = lambda a: jnp.pad(a.reshape(batch, t, -1), ((0, 0), (0, tq - t), (0, 0))).reshape(batch * tq, -1)
        o_dsa = _dsa(qpad(iq), qpad(dq), qpad(iw), key_sets, batch, tq, past, tq)
        o_dsa = o_dsa.reshape(batch, tq, W_DSA)[:, :t].reshape(n, W_DSA)
    else:
        o_dsa = _dsa(iq, dq, iw, key_sets, batch, t, past, tq)

    y, conv_new = _out_ffn(x2d, y_ret, o_dsa, w_out, g_ffn, w_up, w_gate, w_down, conv_w, conv_b,
                           conv_state, g_final, batch, t, min(512, n))
    return (y.reshape(batch, t, d), dk.reshape(batch, t, H_DSA, DH_DSA), dv.reshape(batch, t, H_DSA, DH_DSA),
            ik.reshape(batch, t, D_IDX), ret_new, conv_new)


def kernel(x_prompt, x_sample, cache_dsa_k, cache_dsa_v, cache_idx_k, state_ret, state_ffn_conv, w_in, w_out,
           g_norm_mix, g_gn_ret, g_norm_ffn, w_up, w_gate, w_down, conv_w, conv_b, g_norm_final):
    depth = w_in.shape[0]
    assert depth == 1, "the final norm is fused into the layer's last kernel"
    bp = x_prompt.shape[0]
    d_ff = w_up.shape[-1]
    l = 0
    p_in = w_in.shape[-1]
    pad = (-p_in) % LANES
    weights = (jnp.pad(w_in[l], ((0, 0), (0, pad))).astype(BF16), w_out[l].astype(BF16),
               g_norm_mix[l][None, :], g_gn_ret[l][None, :], g_norm_ffn[l][None, :],
               w_up[l].astype(BF16), w_gate[l].astype(BF16), w_down[l].astype(BF16),
               conv_w[l], conv_b[l][None, :])
    g_final = g_norm_final[None, :]

    yp, pk, pv, pki, pret, pconv = _trunk_layer(
        x_prompt, jnp.zeros((bp, H_RET, DK_RET, DV_RET), F32), None,
        jnp.zeros((bp, CONV_W - 1, d_ff), F32), weights, g_final)
    ys, sk, sv, ski, sret, sconv = _trunk_layer(
        x_sample, state_ret[l], (cache_dsa_k[l], cache_dsa_v[l], cache_idx_k[l]),
        state_ffn_conv[l], weights, g_final)
    st = lambda a: a[None]
    return (yp, ys, st(pk), st(pv), st(pki), st(pret), st(pconv),
            st(sk), st(sv), st(ski), st(sret), st(sconv))
```

```python
import functools

import numpy as np
import jax
import jax.numpy as jnp
from jax import lax
from jax.experimental import pallas as pl
from jax.experimental.pallas import tpu as pltpu

F32 = jnp.float32
BF16 = jnp.bfloat16
I32 = jnp.int32
I16 = jnp.int16
COARSE = jnp.bfloat16

CHUNK = 64
H_RET = 8
DK_RET = 64
DV_RET = 64
H_DSA = 8
DH_DSA = 64
H_IDX = 8
D_IDX = 64
TOPK_MAX = 256
CONV_W = 3
ROPE_BASE = 10000.0
EPS = 1e-6
W_RET = H_RET * DV_RET
W_DSA = H_DSA * DH_DSA

LANES = 128
SUBLANES = 8
HALF_ROWS = 16
DEN_ROWS = 16
STEPS_PER_CHECK = 4
VMEM_LIMIT_BYTES = 56 * 1024 * 1024
INT_MIN = -(2 ** 31)
F32_LOWEST = -3.4028234663852886e38
NEG_BIG = -1e30
LOG2E = 1.4426950408889634
SAFE_DENOM_MIN = 2.0 ** -100
SAFE_DENOM_MAX = 2.0 ** 100


def _dot(a, b):
    return jnp.dot(a, b, preferred_element_type=F32)


def _dot_nt(a, b):
    return lax.dot_general(a, b, (((1,), (1,)), ((), ())), preferred_element_type=F32)


def _dot_tn(a, b):
    return lax.dot_general(a, b, (((0,), (0,)), ((), ())), preferred_element_type=F32)


def _rms(x):
    return x * lax.rsqrt(jnp.mean(x * x, axis=-1, keepdims=True) + EPS)


def _silu(x):
    return x / (1.0 + jnp.exp(-x))


def _inproj_kernel(x_ref, g_ref, w_ref, cos_ref, sin_ref,
                   rq_ref, rk_ref, rv_ref, rg_ref, dq_ref, dk_ref, dv_ref, dkb_ref, dvb_ref,
                   iq_ref, ik_ref, ikb_ref, iw_ref):
    tm = x_ref.shape[0]
    h = (_rms(x_ref[...]) * g_ref[...]).astype(BF16)

    def proj(c0, width):
        return _dot(h, w_ref[:, c0:c0 + width])

    cos = jnp.concatenate([cos_ref[...]] * (W_RET // LANES), axis=1)
    sin = jnp.concatenate([sin_ref[...]] * (W_RET // LANES), axis=1)
    lane = lax.broadcasted_iota(I32, (tm, W_RET), 1)
    first_half = (lane & (DK_RET - 1)) < (DK_RET // 2)

    def rope(p):
        swapped = jnp.where(first_half, pltpu.roll(p, W_RET - DK_RET // 2, 1), pltpu.roll(p, DK_RET // 2, 1))
        return p * cos + swapped * sin

    rq_ref[...] = rope(proj(0, 512)).astype(BF16)
    rk_ref[...] = rope(proj(512, 512)) * (DK_RET ** -0.5)
    rv_ref[...] = proj(1024, 512).astype(BF16)
    rg_ref[...] = proj(1536, 512)
    dq_ref[...] = (proj(2048, 512) * (DH_DSA ** -0.5 * LOG2E)).astype(BF16)
    dk = proj(2560, 512)
    dk_ref[...] = dk
    dkb_ref[...] = dk.astype(BF16)
    dv = proj(3072, 512)
    dv_ref[...] = dv
    dvb_ref[...] = dv.astype(BF16)
    iq_ref[...] = (proj(3584, 512) * (D_IDX ** -0.5)).astype(BF16)
    tail = proj(4096, LANES)
    ik = tail[:, :D_IDX]
    ik_ref[...] = ik
    ikb_ref[...] = ik.astype(BF16)
    iw_ref[...] = tail[:, D_IDX:D_IDX + H_IDX] * (H_IDX ** -0.5)


def _in_proj(x2d, g, w_pad, cos_t, sin_t, tm):
    n, d = x2d.shape
    nt = n // tm
    ntab = cos_t.shape[0] // tm
    row = lambda i: (i, 0)
    const = lambda i: (0, 0)
    tab = lambda i: (i % ntab, 0)
    wide = lambda dt: jax.ShapeDtypeStruct((n, 512), dt)
    out_shape = (wide(BF16), wide(F32), wide(BF16), wide(F32), wide(BF16), wide(F32), wide(F32),
                 wide(BF16), wide(BF16), wide(BF16),
                 jax.ShapeDtypeStruct((n, D_IDX), F32), jax.ShapeDtypeStruct((n, D_IDX), BF16),
                 jax.ShapeDtypeStruct((n, H_IDX), F32))
    out_specs = tuple([pl.BlockSpec((tm, 512), row)] * 10
                      + [pl.BlockSpec((tm, D_IDX), row), pl.BlockSpec((tm, D_IDX), row),
                         pl.BlockSpec((tm, H_IDX), row)])
    return pl.pallas_call(
        _inproj_kernel,
        grid=(nt,),
        in_specs=[pl.BlockSpec((tm, d), row), pl.BlockSpec((1, d), const),
                  pl.BlockSpec(w_pad.shape, const),
                  pl.BlockSpec((tm, LANES), tab), pl.BlockSpec((tm, LANES), tab)],
        out_specs=out_specs,
        out_shape=out_shape,
        compiler_params=pltpu.CompilerParams(dimension_semantics=("arbitrary",),
                                             vmem_limit_bytes=VMEM_LIMIT_BYTES),
        name="in_proj",
    )(x2d, g, w_pad, cos_t, sin_t)


def _ret_kernel(q_ref, k_ref, v_ref, g_ref, s0_ref, dec_ref, xi_ref, zeta_ref, gn_ref,
                y_ref, sout_ref, s_scr, *, gamma_c):
    c = pl.program_id(1)

    @pl.when(c == 0)
    def _():
        s_scr[...] = s0_ref[0]

    q = q_ref[...]
    k = k_ref[...]
    v = v_ref[...]
    kb = k.astype(BF16)
    kz = (k * zeta_ref[...]).astype(BF16)
    xi = xi_ref[...]
    normed = []
    for h in range(H_RET):
        sl = slice(DK_RET * h, DK_RET * (h + 1))
        qh, kh, vh, kzh = q[:, sl], kb[:, sl], v[:, sl], kz[:, sl]
        state = s_scr[h]
        scores = _dot_nt(qh, kh) * dec_ref[h]
        inner = _dot(scores.astype(BF16), vh)
        cross = _dot(qh, state.astype(BF16)) * xi[:, sl]
        o = inner + cross
        s_scr[h] = gamma_c[h] * state + _dot_tn(kzh, vh)
        mu = jnp.mean(o, axis=-1, keepdims=True)
        dev = o - mu
        var = jnp.mean(dev * dev, axis=-1, keepdims=True)
        normed.append(dev * lax.rsqrt(var + EPS))
    on = jnp.concatenate(normed, axis=1) * gn_ref[...]
    y_ref[...] = (_silu(g_ref[...]) * on).astype(BF16)

    @pl.when(c == pl.num_programs(1) - 1)
    def _():
        sout_ref[0] = s_scr[...]


def _retention_constants(c):
    lg = np.log1p(-np.exp2(-5.0 - np.arange(H_RET, dtype=np.float64)))
    idx = np.arange(c, dtype=np.float64)
    diff = idx[:, None] - idx[None, :]
    decay = np.where(diff[None] >= 0, np.exp(lg[:, None, None] * np.maximum(diff, 0.0)[None]), 0.0)
    xi = np.exp(lg[None, :] * (idx + 1.0)[:, None])
    zeta = np.exp(lg[None, :] * (c - 1.0 - idx)[:, None])
    gamma_c = tuple(float(np.float32(g)) for g in np.exp(lg * c))
    rep = lambda a: np.repeat(a, DK_RET, axis=1).astype(np.float32)
    return decay.astype(np.float32), rep(xi), rep(zeta), gamma_c


def _retention(rq, rk, rv, rg, s0, gn, batch, t):
    c = min(2 * CHUNK, t)
    nc = t // c
    decay, xi, zeta, gamma_c = _retention_constants(c)
    blk = lambda b, j: (b * nc + j, 0)
    const2 = lambda b, j: (0, 0)
    const3 = lambda b, j: (0, 0, 0)
    st = lambda b, j: (b, 0, 0, 0)
    n = batch * t
    return pl.pallas_call(
        functools.partial(_ret_kernel, gamma_c=gamma_c),
        grid=(batch, nc),
        in_specs=[pl.BlockSpec((c, W_RET), blk)] * 4
                 + [pl.BlockSpec((1, H_RET, DK_RET, DV_RET), st),
                    pl.BlockSpec((H_RET, c, c), const3),
                    pl.BlockSpec((c, W_RET), const2), pl.BlockSpec((c, W_RET), const2),
                    pl.BlockSpec((1, W_RET), const2)],
        out_specs=(pl.BlockSpec((c, W_RET), blk), pl.BlockSpec((1, H_RET, DK_RET, DV_RET), st)),
        out_shape=(jax.ShapeDtypeStruct((n, W_RET), BF16),
                   jax.ShapeDtypeStruct((batch, H_RET, DK_RET, DV_RET), F32)),
        scratch_shapes=[pltpu.VMEM((H_RET, DK_RET, DV_RET), F32)],
        compiler_params=pltpu.CompilerParams(dimension_semantics=("arbitrary", "arbitrary"),
                                             vmem_limit_bytes=VMEM_LIMIT_BYTES),
        name="retention",
    )(rq, rk, rv, rg, s0, jnp.asarray(decay), jnp.asarray(xi), jnp.asarray(zeta), gn)


def _log2(n):
    assert n > 0 and n & (n - 1) == 0
    return n.bit_length() - 1


def _dsa_kernel(*refs, segs, tq, tq_live, topk, q_pos0):
    nseg = len(segs)
    iq_ref, dq_ref, iw_ref = refs[:3]
    seg_refs = [refs[3 + 3 * s: 6 + 3 * s] for s in range(nseg)]
    o_ref = refs[3 + 3 * nseg]
    sc_scr, half_scr, qi_scr, qm_scr, w_scr, m_scr, acc_scr, kmax_scr = refs[4 + 3 * nseg:]

    chunk_shift = _log2(CHUNK)
    qb = pl.program_id(1)
    q_first = q_pos0 + qb * tq
    q_last_chunk = lax.shift_right_logical(q_first + tq - 1, chunk_shift)
    q_chunk = lax.shift_right_logical(q_first + lax.broadcasted_iota(I32, (1, tq), 1), chunk_shift)
    live = lax.broadcasted_iota(I32, (1, tq), 1) < tq_live

    iq_t = iq_ref[...].astype(F32).T
    dq_t = dq_ref[...].astype(F32).T
    w_scr[...] = iw_ref[...].T
    pair_row = lax.broadcasted_iota(I32, (LANES, tq), 0)
    for h in range(H_IDX):
        qi_scr[h] = iq_t[D_IDX * h:D_IDX * (h + 1)].astype(BF16)
    for hp in range(H_DSA // 2):
        pair = dq_t[LANES * hp:LANES * (hp + 1)]
        qm_scr[hp] = jnp.concatenate([jnp.where(pair_row < DH_DSA, pair, 0.0),
                                      jnp.where(pair_row >= DH_DSA, pair, 0.0)], axis=1).astype(BF16)
    acc_scr[...] = jnp.zeros(acc_scr.shape, F32)

    def blocks_needed(seg):
        valid, blk, _, pos0 = seg
        adm = jnp.clip((q_last_chunk + 1) * CHUNK - pos0, 0, valid)
        return lax.shift_right_logical(adm + blk - 1, _log2(blk))

    def rows_of(seg, j):
        _, blk, row0, _ = seg
        return pl.ds(pl.multiple_of(row0 + j * blk, blk), blk)

    for seg, (kidx_ref, _, _) in zip(segs, seg_refs):
        valid, blk, _, pos0 = seg

        def score_block(j, carry, seg=seg, kidx_ref=kidx_ref, valid=valid, blk=blk, pos0=pos0):
            kidx = kidx_ref[0, pl.ds(pl.multiple_of(j * blk, blk), blk), :].astype(BF16)
            tot = jnp.zeros((blk, tq), F32)
            for h in range(H_IDX):
                tot = tot + jnp.maximum(_dot(kidx, qi_scr[h]), 0.0) * w_scr[h:h + 1, :]
            idx = j * blk + lax.broadcasted_iota(I32, (blk, tq), 0)
            ok = (lax.shift_right_logical(pos0 + idx, chunk_shift) <= q_chunk) & (idx < valid)
            score = jnp.where(ok, tot, -jnp.inf)
            sc_scr[rows_of(seg, j), :] = score
            near = score.astype(COARSE)
            bits = pltpu.bitcast(near, I16)
            below = pltpu.bitcast(bits + jnp.where(bits < 0, jnp.int16(1), jnp.int16(-1)), COARSE)
            half_scr[rows_of(seg, j), :] = jnp.where(near.astype(F32) > score, below, near)
            return carry

        lax.fori_loop(0, blocks_needed(seg), score_block, 0)

    def fold_rows(a, rows):
        parts = [a[rows * r:rows * (r + 1)] for r in range(a.shape[0] // rows)]
        while len(parts) > 1:
            parts = [parts[i] + parts[i + 1] for i in range(0, len(parts) - 1, 2)] + parts[len(parts) & ~1:]
        return parts[0]

    def count_rows(scr, rows, one, hit_fn):
        cnt = jnp.zeros((rows, tq), one.dtype)
        for seg in segs:
            blk, row0 = seg[1], seg[2]

            def body(j, cnt, seg=seg, blk=blk, row0=row0):
                v = scr[rows_of(seg, j), :]
                sub = 16 * rows
                for c in range(0, blk, sub):
                    hit = hit_fn(v[c:c + sub], row0 + j * blk + c)
                    cnt = cnt + fold_rows(jnp.where(hit, one, jnp.zeros_like(one)), rows)
                return cnt

            cnt = lax.fori_loop(0, blocks_needed(seg), body, cnt)
        return jnp.sum(cnt.astype(I32), axis=0, keepdims=True)

    def key_to_f32(key):
        return pltpu.bitcast(jnp.where(key < 0, key ^ jnp.int32(0x7FFFFFFF), key), F32)

    def coarse_bits(key16):
        return jnp.where(key16 < 0, key16 ^ jnp.int32(0x7FFF), key16)

    def largest_passing(n_all, count_ge):
        def unresolved(carry):
            i, _, n_ans = carry
            return (i < 16) & (jnp.max(jnp.where(live & (n_ans != topk), 1, 0)) > 0)

        def steps(carry):
            i, ans, n_ans = carry
            for _ in range(STEPS_PER_CHECK):
                cand = jnp.where(i == 0, jnp.zeros_like(ans), ans | jnp.left_shift(jnp.int32(1), 15 - i))
                n = count_ge(cand)
                take = n >= topk
                i, ans, n_ans = i + 1, jnp.where(take, cand, ans), jnp.where(take, n, n_ans)
            return i, ans, n_ans

        _, ans, n_ans = lax.while_loop(unresolved, steps, (jnp.int32(0), jnp.full((1, tq), -32768, I32), n_all))
        return ans, n_ans

    total = jnp.zeros((1, tq), I32)
    for seg in segs:
        total = total + blocks_needed(seg) * seg[1]

    one16 = jnp.ones((1, 1), I16)
    one32 = jnp.ones((1, 1), I32)

    def count_coarse_ge(key16):
        cand = pltpu.bitcast(coarse_bits(key16).astype(I16), COARSE)
        return count_rows(half_scr, HALF_ROWS, one16, lambda v, r0: v >= cand)

    key16, n_coarse = largest_passing(total, count_coarse_ge)
    short = key16 == -32768
    base = lax.shift_left(coarse_bits(key16), 16)
    base = jnp.where(base < 0, base ^ jnp.int32(0x7FFFFFFF), base)

    def count_fine_ge(off):
        cand = key_to_f32(base + off + 32768)
        return count_rows(sc_scr, SUBLANES, one32, lambda v, r0: v >= cand)

    off, n_ge = largest_passing(n_coarse, count_fine_ge)
    thr = jnp.where(short, -jnp.inf, key_to_f32(jnp.where(short, 0, base + off + 32768)))

    need = live & (n_ge > topk) & jnp.logical_not(short)

    @pl.when(jnp.max(jnp.where(need, 1, 0)) > 0)
    def _():
        def count_scores(hit_fn):
            def with_rows(v, r0):
                return hit_fn(v, r0 + lax.broadcasted_iota(I32, v.shape, 0))

            return count_rows(sc_scr, SUBLANES, one32, with_rows)

        quota = topk - count_scores(lambda v, ridx: v > thr)
        nbits = int(sc_scr.shape[0]).bit_length()

        def index_bisect(i, last):
            cand = last | jnp.left_shift(jnp.int32(1), nbits - 1 - i)
            below = count_scores(lambda v, ridx: (v == thr) & (ridx < cand))
            return jnp.where(below < quota, cand, last)

        last = lax.fori_loop(0, nbits, index_bisect, jnp.zeros((1, tq), I32))

        for seg in segs:
            blk, row0 = seg[1], seg[2]

            def drop(j, carry, seg=seg, blk=blk, row0=row0):
                v = sc_scr[rows_of(seg, j), :]
                ridx = row0 + j * blk + lax.broadcasted_iota(I32, (blk, tq), 0)
                cut = need & (v == thr) & (ridx > last)
                sc_scr[rows_of(seg, j), :] = jnp.where(cut, -jnp.inf, v)
                return carry

            lax.fori_loop(0, blocks_needed(seg), drop, 0)

    thr_sel = jnp.maximum(thr, F32_LOWEST)

    def attend_all(body):
        for seg, (_, k_ref, v_ref) in zip(segs, seg_refs):
            blk = seg[1]

            def step(j, carry, seg=seg, k_ref=k_ref, v_ref=v_ref, blk=blk):
                bias = jnp.where(sc_scr[rows_of(seg, j), :] >= thr_sel, 0.0, NEG_BIG)
                bias = jnp.concatenate([bias, bias], axis=1)
                rows = pl.ds(pl.multiple_of(j * blk, blk), blk)
                kblk = k_ref[0, rows, :].astype(BF16)
                vblk = v_ref[0, rows, :].astype(F32)
                ones = jnp.ones((DEN_ROWS, blk), BF16)
                for hp in range(H_DSA // 2):
                    lanes = slice(LANES * hp, LANES * (hp + 1))
                    s = _dot(kblk[:, lanes], qm_scr[hp]) + bias
                    v_aug = jnp.concatenate([vblk[:, lanes].T.astype(BF16), ones], axis=0)
                    body(hp, s, v_aug)
                return carry

            lax.fori_loop(0, blocks_needed(seg), step, 0)

    @pl.when(qb == 0)
    def _():
        feat = lax.shift_right_logical(lax.broadcasted_iota(I32, (W_DSA, LANES), 0), _log2(DH_DSA))
        head_sum = jnp.where(feat == lax.broadcasted_iota(I32, (W_DSA, LANES), 1), 1.0, 0.0).astype(BF16)
        best = jnp.zeros((1, LANES), F32)
        for seg, (_, k_ref, _) in zip(segs, seg_refs):
            blk = seg[1]

            def norms(j, best, k_ref=k_ref, blk=blk):
                kb = k_ref[0, pl.ds(pl.multiple_of(j * blk, blk), blk), :].astype(F32)
                sq = _dot((kb * kb).astype(BF16), head_sum)
                return jnp.maximum(best, jnp.max(sq, axis=0, keepdims=True))

            best = lax.fori_loop(0, k_ref.shape[1] // blk, norms, best)
        knorm = jnp.sqrt(best)
        for h in range(H_DSA):
            kmax_scr[h // 2, :, tq * (h % 2):tq * (h % 2 + 1)] = jnp.broadcast_to(knorm[:, h:h + 1], (1, tq))

    for hp in range(H_DSA // 2):
        qp = qm_scr[hp].astype(F32)
        m_scr[hp] = jnp.sqrt(jnp.sum(qp * qp, axis=0, keepdims=True)) * kmax_scr[hp]

    def fast_pair(hp, s, v_aug):
        acc_scr[hp] = acc_scr[hp] + _dot(v_aug, jnp.exp2(s - m_scr[hp]).astype(BF16))

    attend_all(fast_pair)

    bad = jnp.zeros((1, 2 * tq), I32)
    for hp in range(H_DSA // 2):
        den = acc_scr[hp][LANES:LANES + 1]
        bad = bad | jnp.where((den >= SAFE_DENOM_MIN) & (den <= SAFE_DENOM_MAX), 0, 1)

    @pl.when(jnp.max(bad) > 0)
    def _():
        m_scr[...] = jnp.full(m_scr.shape, NEG_BIG, F32)
        acc_scr[...] = jnp.zeros(acc_scr.shape, F32)

        def online_pair(hp, s, v_aug):
            m_old = m_scr[hp]
            m_new = jnp.maximum(m_old, jnp.max(s, axis=0, keepdims=True))
            acc_scr[hp] = (jnp.exp2(m_old - m_new) * acc_scr[hp]
                           + _dot(v_aug, jnp.exp2(s - m_new).astype(BF16)))
            m_scr[hp] = m_new

        attend_all(online_pair)

    for hp in range(H_DSA // 2):
        acc = acc_scr[hp]
        den = acc[LANES:LANES + 1]
        pair_t = jnp.concatenate([acc[:DH_DSA, :tq] / den[:, :tq], acc[DH_DSA:LANES, tq:] / den[:, tq:]], axis=0)
        o_ref[:, LANES * hp:LANES * (hp + 1)] = pair_t.T.astype(BF16)


def _dsa(iq, dq, iw, key_sets, batch, t, past, tq, tq_live):
    nq = t // tq
    ltot = sum(ks[0] for ks in key_sets)
    topk = min(TOPK_MAX, ltot // 4)
    segs = []
    row = 0
    pos = 0
    for valid, kidx, _, _ in key_sets:
        length = kidx.shape[1]
        blk = min(512, length)
        assert blk % LANES == 0 and length % blk == 0 and valid <= length
        segs.append((valid, blk, row, pos))
        row += length
        pos += valid
    assert topk <= row and tq % LANES == 0
    qrow = lambda b, i: (b * nq + i, 0)
    per_b = lambda b, i: (b, 0, 0)
    in_specs = [pl.BlockSpec((tq, 512), qrow), pl.BlockSpec((tq, 512), qrow), pl.BlockSpec((tq, H_IDX), qrow)]
    operands = [iq, dq, iw]
    for _, kidx, k, v in key_sets:
        length = kidx.shape[1]
        in_specs += [pl.BlockSpec((1, length, D_IDX), per_b), pl.BlockSpec((1, length, W_DSA), per_b),
                     pl.BlockSpec((1, length, W_DSA), per_b)]
        operands += [kidx, k, v]
    return pl.pallas_call(
        functools.partial(_dsa_kernel, segs=tuple(segs), tq=tq, tq_live=tq_live, topk=topk, q_pos0=past),
        grid=(batch, nq),
        in_specs=in_specs,
        out_specs=pl.BlockSpec((tq, W_DSA), qrow),
        out_shape=jax.ShapeDtypeStruct((batch * t, W_DSA), BF16),
        scratch_shapes=[pltpu.VMEM((row, tq), F32),
                        pltpu.VMEM((row, tq), COARSE),
                        pltpu.VMEM((H_IDX, D_IDX, tq), BF16),
                        pltpu.VMEM((H_DSA // 2, LANES, 2 * tq), BF16),
                        pltpu.VMEM((H_IDX, tq), F32),
                        pltpu.VMEM((H_DSA // 2, 1, 2 * tq), F32),
                        pltpu.VMEM((H_DSA // 2, LANES + DEN_ROWS, 2 * tq), F32),
                        pltpu.VMEM((H_DSA // 2, 1, 2 * tq), F32)],
        compiler_params=pltpu.CompilerParams(dimension_semantics=("arbitrary", "arbitrary"),
                                             vmem_limit_bytes=VMEM_LIMIT_BYTES),
        name="dsa",
    )(*operands)


def _ffn_kernel(x_ref, yr_ref, od_ref, wo_ref, g2_ref, wup_ref, wg_ref, wd_ref, cw_ref, cb_ref, cs_ref, gf_ref,
                y_ref, cnew_ref, carry_scr, *, fc):
    t = pl.program_id(1)
    tm = x_ref.shape[0]
    d_ff = wup_ref.shape[1]
    nseq = carry_scr.shape[0]
    seq = tm // nseq

    @pl.when(t == 0)
    def _():
        carry_scr[...] = cs_ref[...]

    x1 = x_ref[...] + _dot(yr_ref[...], wo_ref[:W_RET, :]) + _dot(od_ref[...], wo_ref[W_RET:, :])
    h2 = (_rms(x1) * g2_ref[...]).astype(BF16)
    row = lax.broadcasted_iota(I32, (tm, fc), 0) & (seq - 1)
    acc = jnp.zeros(x1.shape, F32)

    def carried(r, cols):
        if nseq == 1:
            return carry_scr[0, r:r + 1, cols]
        return jnp.concatenate([jnp.broadcast_to(carry_scr[s, r:r + 1, cols], (seq, fc)) for s in range(nseq)], axis=0)

    for c0 in range(0, d_ff, fc):
        cols = slice(c0, c0 + fc)
        a = _dot(h2, wup_ref[:, cols])
        u = _dot(h2, wg_ref[:, cols])
        prev2, prev1 = carried(0, cols), carried(1, cols)
        a1 = jnp.where(row == 0, prev1, pltpu.roll(a, 1, 0))
        a2 = jnp.where(row == 0, prev2, jnp.where(row == 1, prev1, pltpu.roll(a, 2, 0)))
        cw = cw_ref[:, cols]
        conv = cb_ref[:, cols] + a2 * cw[0:1] + a1 * cw[1:2] + a * cw[2:3]
        for s in range(nseq):
            carry_scr[s, :, cols] = a[(s + 1) * seq - 2:(s + 1) * seq, :]
        acc = acc + _dot((_silu(conv) * u).astype(BF16), wd_ref[cols, :])
    y_ref[...] = _rms(x1 + acc) * gf_ref[...]

    @pl.when(t == pl.num_programs(1) - 1)
    def _():
        cnew_ref[...] = carry_scr[...]


def _out_ffn(x2d, y_ret, o_dsa, w_out, g2, w_up, w_gate, w_down, conv_w, conv_b, conv_state, g_final,
             batch, t, tm):
    n, d = x2d.shape
    d_ff = w_up.shape[1]
    nseq = max(1, tm // t)
    assert tm % nseq == 0 and (tm // nseq) & (tm // nseq - 1) == 0 and batch % nseq == 0 and t % (tm // nseq) == 0
    nt = t * nseq // tm
    fc = 256
    assert d_ff % fc == 0
    row = lambda g, i: (g * nt + i, 0)
    const = lambda g, i: (0, 0)
    per_g = lambda g, i: (g, 0, 0)
    return pl.pallas_call(
        functools.partial(_ffn_kernel, fc=fc),
        grid=(batch // nseq, nt),
        in_specs=[pl.BlockSpec((tm, d), row), pl.BlockSpec((tm, W_RET), row), pl.BlockSpec((tm, W_DSA), row),
                  pl.BlockSpec(w_out.shape, const), pl.BlockSpec((1, d), const),
                  pl.BlockSpec(w_up.shape, const), pl.BlockSpec(w_gate.shape, const),
                  pl.BlockSpec(w_down.shape, const), pl.BlockSpec(conv_w.shape, const),
                  pl.BlockSpec((1, d_ff), const), pl.BlockSpec((nseq, CONV_W - 1, d_ff), per_g),
                  pl.BlockSpec((1, d), const)],
        out_specs=(pl.BlockSpec((tm, d), row), pl.BlockSpec((nseq, CONV_W - 1, d_ff), per_g)),
        out_shape=(jax.ShapeDtypeStruct((n, d), F32), jax.ShapeDtypeStruct((batch, CONV_W - 1, d_ff), F32)),
        scratch_shapes=[pltpu.VMEM((nseq, CONV_W - 1, d_ff), F32)],
        compiler_params=pltpu.CompilerParams(dimension_semantics=("arbitrary", "arbitrary"),
                                             vmem_limit_bytes=VMEM_LIMIT_BYTES),
        name="out_ffn",
    )(x2d, y_ret, o_dsa, w_out, g2, w_up, w_gate, w_down, conv_w, conv_b, conv_state, g_final)


def _rope_tables(past, t, rows):
    half = DK_RET // 2
    inv_freq = ROPE_BASE ** (-jnp.arange(half, dtype=F32) / half)
    pos = past + jnp.arange(t, dtype=jnp.int32)
    ang = pos.astype(F32)[:, None] * inv_freq[None, :]
    cos, sin = jnp.cos(ang), jnp.sin(ang)
    cos_t = jnp.concatenate([cos, cos, cos, cos], axis=1)
    sin_t = jnp.concatenate([-sin, sin, -sin, sin], axis=1)
    if t < rows:
        cos_t = jnp.tile(cos_t, (rows // t, 1))
        sin_t = jnp.tile(sin_t, (rows // t, 1))
    return cos_t, sin_t


def _trunk_layer(x, ret_state, past_kv, conv_state, weights, g_final):
    w_in_pad, w_out, g_mix, g_gn, g_ffn, w_up, w_gate, w_down, conv_w, conv_b = weights
    batch, t, d = x.shape
    n = batch * t
    past = 0 if past_kv is None else past_kv[0].shape[1]
    x2d = x.reshape(n, d)

    tm_in = min(512, n)
    cos_t, sin_t = _rope_tables(past, t, tm_in)
    (rq, rk, rv, rg, dq, dk, dv, dkb, dvb, iq, ik, ikb, iw) = _in_proj(x2d, g_mix, w_in_pad, cos_t, sin_t, tm_in)

    y_ret, ret_new = _retention(rq, rk, rv, rg, ret_state, g_gn, batch, t)

    if past_kv is None:
        key_sets = [(t, ikb.reshape(batch, t, D_IDX), dkb.reshape(batch, t, W_DSA), dvb.reshape(batch, t, W_DSA))]
    else:
        past_k, past_v, past_ki = past_kv
        rows = -(-t // LANES) * LANES
        padded = lambda a, w: jnp.pad(a.reshape(batch, t, w), ((0, 0), (0, rows - t), (0, 0)))
        key_sets = [(past, past_ki, past_k.reshape(batch, past, W_DSA), past_v.reshape(batch, past, W_DSA)),
                    (t, padded(ikb, D_IDX), padded(dkb, W_DSA), padded(dvb, W_DSA))]
    tq = min(256, -(-t // LANES) * LANES)
    if t % tq:
        qpad = lambda a: jnp.pad(a.reshape(batch, t, -1), ((0, 0), (0, tq - t), (0, 0))).reshape(batch * tq, -1)
        o_dsa = _dsa(qpad(iq), qpad(dq), qpad(iw), key_sets, batch, tq, past, tq, t)
        o_dsa = o_dsa.reshape(batch, tq, W_DSA)[:, :t].reshape(n, W_DSA)
    else:
        o_dsa = _dsa(iq, dq, iw, key_sets, batch, t, past, tq, tq)

    y, conv_new = _out_ffn(x2d, y_ret, o_dsa, w_out, g_ffn, w_up, w_gate, w_down, conv_w, conv_b,
                           conv_state, g_final, batch, t, min(512, n))
    return (y.reshape(batch, t, d), dk.reshape(batch, t, H_DSA, DH_DSA), dv.reshape(batch, t, H_DSA, DH_DSA),
            ik.reshape(batch, t, D_IDX), ret_new, conv_new)


def kernel(x_prompt, x_sample, cache_dsa_k, cache_dsa_v, cache_idx_k, state_ret, state_ffn_conv, w_in, w_out,
           g_norm_mix, g_gn_ret, g_norm_ffn, w_up, w_gate, w_down, conv_w, conv_b, g_norm_final):
    depth = w_in.shape[0]
    assert depth == 1, "the final norm is fused into the layer's last kernel"
    bp = x_prompt.shape[0]
    d_ff = w_up.shape[-1]
    l = 0
    p_in = w_in.shape[-1]
    pad = (-p_in) % LANES
    weights = (jnp.pad(w_in[l], ((0, 0), (0, pad))).astype(BF16), w_out[l].astype(BF16),
               g_norm_mix[l][None, :], g_gn_ret[l][None, :], g_norm_ffn[l][None, :],
               w_up[l].astype(BF16), w_gate[l].astype(BF16), w_down[l].astype(BF16),
               conv_w[l], conv_b[l][None, :])
    g_final = g_norm_final[None, :]

    yp, pk, pv, pki, pret, pconv = _trunk_layer(
        x_prompt, jnp.zeros((bp, H_RET, DK_RET, DV_RET), F32), None,
        jnp.zeros((bp, CONV_W - 1, d_ff), F32), weights, g_final)
    ys, sk, sv, ski, sret, sconv = _trunk_layer(
        x_sample, state_ret[l], (cache_dsa_k[l], cache_dsa_v[l], cache_idx_k[l]),
        state_ffn_conv[l], weights, g_final)
    st = lambda a: a[None]
    return (yp, ys, st(pk), st(pv), st(pki), st(pret), st(pconv),
            st(sk), st(sv), st(ski), st(sret), st(sconv))
```

```python
import functools

import numpy as np
import jax
import jax.numpy as jnp
from jax import lax
from jax.experimental import pallas as pl
from jax.experimental.pallas import tpu as pltpu

F32 = jnp.float32
BF16 = jnp.bfloat16
I32 = jnp.int32
I16 = jnp.int16
COARSE = jnp.bfloat16

CHUNK = 64
H_RET = 8
DK_RET = 64
DV_RET = 64
H_DSA = 8
DH_DSA = 64
H_IDX = 8
D_IDX = 64
TOPK_MAX = 256
CONV_W = 3
ROPE_BASE = 10000.0
EPS = 1e-6
W_RET = H_RET * DV_RET
W_DSA = H_DSA * DH_DSA

LANES = 128
SUBLANES = 8
HALF_ROWS = 16
DEN_ROWS = 16
VMEM_LIMIT_BYTES = 56 * 1024 * 1024
INT_MIN = -(2 ** 31)
F32_LOWEST = -3.4028234663852886e38
NEG_BIG = -1e30
LOG2E = 1.4426950408889634
SAFE_DENOM_MIN = 2.0 ** -100
SAFE_DENOM_MAX = 2.0 ** 100


def _dot(a, b):
    return jnp.dot(a, b, preferred_element_type=F32)


def _dot_nt(a, b):
    return lax.dot_general(a, b, (((1,), (1,)), ((), ())), preferred_element_type=F32)


def _dot_tn(a, b):
    return lax.dot_general(a, b, (((0,), (0,)), ((), ())), preferred_element_type=F32)


def _rms(x):
    return x * lax.rsqrt(jnp.mean(x * x, axis=-1, keepdims=True) + EPS)


def _silu(x):
    return x / (1.0 + jnp.exp(-x))


def _inproj_kernel(x_ref, g_ref, w_ref, cos_ref, sin_ref,
                   rq_ref, rk_ref, rv_ref, rg_ref, dq_ref, dk_ref, dv_ref, dkb_ref, dvb_ref,
                   iq_ref, ik_ref, ikb_ref, iw_ref):
    tm = x_ref.shape[0]
    h = (_rms(x_ref[...]) * g_ref[...]).astype(BF16)

    def proj(c0, width):
        return _dot(h, w_ref[:, c0:c0 + width])

    cos = jnp.concatenate([cos_ref[...]] * (W_RET // LANES), axis=1)
    sin = jnp.concatenate([sin_ref[...]] * (W_RET // LANES), axis=1)
    lane = lax.broadcasted_iota(I32, (tm, W_RET), 1)
    first_half = (lane & (DK_RET - 1)) < (DK_RET // 2)

    def rope(p):
        swapped = jnp.where(first_half, pltpu.roll(p, W_RET - DK_RET // 2, 1), pltpu.roll(p, DK_RET // 2, 1))
        return p * cos + swapped * sin

    rq_ref[...] = rope(proj(0, 512)).astype(BF16)
    rk_ref[...] = rope(proj(512, 512)) * (DK_RET ** -0.5)
    rv_ref[...] = proj(1024, 512).astype(BF16)
    rg_ref[...] = proj(1536, 512)
    dq_ref[...] = (proj(2048, 512) * (DH_DSA ** -0.5 * LOG2E)).astype(BF16)
    dk = proj(2560, 512)
    dk_ref[...] = dk
    dkb_ref[...] = dk.astype(BF16)
    dv = proj(3072, 512)
    dv_ref[...] = dv
    dvb_ref[...] = dv.astype(BF16)
    iq_ref[...] = (proj(3584, 512) * (D_IDX ** -0.5)).astype(BF16)
    tail = proj(4096, LANES)
    ik = tail[:, :D_IDX]
    ik_ref[...] = ik
    ikb_ref[...] = ik.astype(BF16)
    iw_ref[...] = tail[:, D_IDX:D_IDX + H_IDX] * (H_IDX ** -0.5)


def _in_proj(x2d, g, w_pad, cos_t, sin_t, tm):
    n, d = x2d.shape
    nt = n // tm
    ntab = cos_t.shape[0] // tm
    row = lambda i: (i, 0)
    const = lambda i: (0, 0)
    tab = lambda i: (i % ntab, 0)
    wide = lambda dt: jax.ShapeDtypeStruct((n, 512), dt)
    out_shape = (wide(BF16), wide(F32), wide(BF16), wide(F32), wide(BF16), wide(F32), wide(F32),
                 wide(BF16), wide(BF16), wide(BF16),
                 jax.ShapeDtypeStruct((n, D_IDX), F32), jax.ShapeDtypeStruct((n, D_IDX), BF16),
                 jax.ShapeDtypeStruct((n, H_IDX), F32))
    out_specs = tuple([pl.BlockSpec((tm, 512), row)] * 10
                      + [pl.BlockSpec((tm, D_IDX), row), pl.BlockSpec((tm, D_IDX), row),
                         pl.BlockSpec((tm, H_IDX), row)])
    return pl.pallas_call(
        _inproj_kernel,
        grid=(nt,),
        in_specs=[pl.BlockSpec((tm, d), row), pl.BlockSpec((1, d), const),
                  pl.BlockSpec(w_pad.shape, const),
                  pl.BlockSpec((tm, LANES), tab), pl.BlockSpec((tm, LANES), tab)],
        out_specs=out_specs,
        out_shape=out_shape,
        compiler_params=pltpu.CompilerParams(dimension_semantics=("arbitrary",),
                                             vmem_limit_bytes=VMEM_LIMIT_BYTES),
        name="in_proj",
    )(x2d, g, w_pad, cos_t, sin_t)


def _ret_kernel(q_ref, k_ref, v_ref, g_ref, s0_ref, dec_ref, xi_ref, zeta_ref, gn_ref,
                y_ref, sout_ref, s_scr, *, gamma_c):
    c = pl.program_id(1)
    chunk = q_ref.shape[0]
    npair = H_RET // 2
    blk_row = lax.broadcasted_iota(I32, (LANES, LANES), 0) < DK_RET
    blk_col = lax.broadcasted_iota(I32, (LANES, LANES), 1) < DV_RET
    own_block = blk_row == blk_col
    low = lax.broadcasted_iota(I32, (chunk, LANES), 1) < DV_RET

    @pl.when(c == 0)
    def _():
        for p in range(npair):
            s_scr[p] = jnp.zeros((LANES, LANES), F32)
            s_scr[p, :DK_RET, :DV_RET] = s0_ref[0, 2 * p]
            s_scr[p, DK_RET:, DV_RET:] = s0_ref[0, 2 * p + 1]

    q = q_ref[...]
    k = k_ref[...]
    v = v_ref[...]
    kb = k.astype(BF16)
    kz = (k * zeta_ref[...]).astype(BF16)
    xi = xi_ref[...]
    normed = []
    for p in range(npair):
        sl = slice(LANES * p, LANES * (p + 1))
        qp, kp, vp, kzp = q[:, sl], kb[:, sl], v[:, sl], kz[:, sl]
        state = s_scr[p]
        q2 = jnp.concatenate([jnp.where(low, qp, jnp.zeros_like(qp)), jnp.where(low, jnp.zeros_like(qp), qp)], axis=0)
        scores = _dot_nt(q2, kp) * dec_ref[p]
        both = _dot(scores.astype(BF16), vp)
        inner = jnp.where(low, both[:chunk], both[chunk:])
        cross = _dot(qp, state.astype(BF16)) * xi[:, sl]
        o = inner + cross
        decay_rows = jnp.where(blk_row, gamma_c[2 * p], gamma_c[2 * p + 1])
        s_scr[p] = decay_rows * state + jnp.where(own_block, _dot_tn(kzp, vp), 0.0)
        inv = 1.0 / DV_RET
        mu = jnp.where(low, jnp.sum(jnp.where(low, o, 0.0), axis=-1, keepdims=True),
                       jnp.sum(jnp.where(low, 0.0, o), axis=-1, keepdims=True)) * inv
        dev = o - mu
        sq = dev * dev
        var = jnp.where(low, jnp.sum(jnp.where(low, sq, 0.0), axis=-1, keepdims=True),
                        jnp.sum(jnp.where(low, 0.0, sq), axis=-1, keepdims=True)) * inv
        normed.append(dev * lax.rsqrt(var + EPS))
    on = jnp.concatenate(normed, axis=1) * gn_ref[...]
    y_ref[...] = (_silu(g_ref[...]) * on).astype(BF16)

    @pl.when(c == pl.num_programs(1) - 1)
    def _():
        for p in range(npair):
            sout_ref[0, 2 * p] = s_scr[p, :DK_RET, :DV_RET]
            sout_ref[0, 2 * p + 1] = s_scr[p, DK_RET:, DV_RET:]


def _retention_constants(c):
    lg = np.log1p(-np.exp2(-5.0 - np.arange(H_RET, dtype=np.float64)))
    idx = np.arange(c, dtype=np.float64)
    diff = idx[:, None] - idx[None, :]
    decay = np.where(diff[None] >= 0, np.exp(lg[:, None, None] * np.maximum(diff, 0.0)[None]), 0.0)
    xi = np.exp(lg[None, :] * (idx + 1.0)[:, None])
    zeta = np.exp(lg[None, :] * (c - 1.0 - idx)[:, None])
    gamma_c = tuple(float(np.float32(g)) for g in np.exp(lg * c))
    rep = lambda a: np.repeat(a, DK_RET, axis=1).astype(np.float32)
    pair_decay = decay.reshape(H_RET // 2, 2 * c, c)
    return pair_decay.astype(np.float32), rep(xi), rep(zeta), gamma_c


def _retention(rq, rk, rv, rg, s0, gn, batch, t):
    c = min(4 * CHUNK, t)
    nc = t // c
    decay, xi, zeta, gamma_c = _retention_constants(c)
    blk = lambda b, j: (b * nc + j, 0)
    const2 = lambda b, j: (0, 0)
    const3 = lambda b, j: (0, 0, 0)
    st = lambda b, j: (b, 0, 0, 0)
    n = batch * t
    return pl.pallas_call(
        functools.partial(_ret_kernel, gamma_c=gamma_c),
        grid=(batch, nc),
        in_specs=[pl.BlockSpec((c, W_RET), blk)] * 4
                 + [pl.BlockSpec((1, H_RET, DK_RET, DV_RET), st),
                    pl.BlockSpec((H_RET // 2, 2 * c, c), const3),
                    pl.BlockSpec((c, W_RET), const2), pl.BlockSpec((c, W_RET), const2),
                    pl.BlockSpec((1, W_RET), const2)],
        out_specs=(pl.BlockSpec((c, W_RET), blk), pl.BlockSpec((1, H_RET, DK_RET, DV_RET), st)),
        out_shape=(jax.ShapeDtypeStruct((n, W_RET), BF16),
                   jax.ShapeDtypeStruct((batch, H_RET, DK_RET, DV_RET), F32)),
        scratch_shapes=[pltpu.VMEM((H_RET // 2, LANES, LANES), F32)],
        compiler_params=pltpu.CompilerParams(dimension_semantics=("arbitrary", "arbitrary"),
                                             vmem_limit_bytes=VMEM_LIMIT_BYTES),
        name="retention",
    )(rq, rk, rv, rg, s0, jnp.asarray(decay), jnp.asarray(xi), jnp.asarray(zeta), gn)


def _log2(n):
    assert n > 0 and n & (n - 1) == 0
    return n.bit_length() - 1


def _dsa_kernel(*refs, segs, tq, tq_live, topk, q_pos0):
    nseg = len(segs)
    iq_ref, dq_ref, iw_ref = refs[:3]
    seg_refs = [refs[3 + 3 * s: 6 + 3 * s] for s in range(nseg)]
    o_ref = refs[3 + 3 * nseg]
    sc_scr, half_scr, qi_scr, qm_scr, w_scr, m_scr, acc_scr, kmax_scr = refs[4 + 3 * nseg:]

    chunk_shift = _log2(CHUNK)
    qb = pl.program_id(1)
    q_first = q_pos0 + qb * tq
    q_last_chunk = lax.shift_right_logical(q_first + tq - 1, chunk_shift)
    q_chunk = lax.shift_right_logical(q_first + lax.broadcasted_iota(I32, (1, tq), 1), chunk_shift)
    live = lax.broadcasted_iota(I32, (1, tq), 1) < tq_live

    iq_t = iq_ref[...].astype(F32).T
    dq_t = dq_ref[...].astype(F32).T
    w_scr[...] = iw_ref[...].T
    pair_row = lax.broadcasted_iota(I32, (LANES, tq), 0)
    for h in range(H_IDX):
        qi_scr[h] = iq_t[D_IDX * h:D_IDX * (h + 1)].astype(BF16)
    for hp in range(H_DSA // 2):
        pair = dq_t[LANES * hp:LANES * (hp + 1)]
        qm_scr[hp] = jnp.concatenate([jnp.where(pair_row < DH_DSA, pair, 0.0),
                                      jnp.where(pair_row >= DH_DSA, pair, 0.0)], axis=1).astype(BF16)
    acc_scr[...] = jnp.zeros(acc_scr.shape, F32)

    def blocks_needed(seg):
        valid, blk, _, pos0 = seg
        adm = jnp.clip((q_last_chunk + 1) * CHUNK - pos0, 0, valid)
        return lax.shift_right_logical(adm + blk - 1, _log2(blk))

    def rows_of(seg, j):
        _, blk, row0, _ = seg
        return pl.ds(pl.multiple_of(row0 + j * blk, blk), blk)

    for seg, (kidx_ref, _, _) in zip(segs, seg_refs):
        valid, blk, _, pos0 = seg

        def score_block(j, carry, seg=seg, kidx_ref=kidx_ref, valid=valid, blk=blk, pos0=pos0):
            kidx = kidx_ref[0, pl.ds(pl.multiple_of(j * blk, blk), blk), :].astype(BF16)
            tot = jnp.zeros((blk, tq), F32)
            for h in range(H_IDX):
                tot = tot + jnp.maximum(_dot(kidx, qi_scr[h]), 0.0) * w_scr[h:h + 1, :]
            idx = j * blk + lax.broadcasted_iota(I32, (blk, tq), 0)
            ok = (lax.shift_right_logical(pos0 + idx, chunk_shift) <= q_chunk) & (idx < valid)
            score = jnp.where(ok, tot, -jnp.inf)
            sc_scr[rows_of(seg, j), :] = score
            near = score.astype(COARSE)
            bits = pltpu.bitcast(near, I16)
            below = pltpu.bitcast(bits + jnp.where(bits < 0, jnp.int16(1), jnp.int16(-1)), COARSE)
            half_scr[rows_of(seg, j), :] = jnp.where(near.astype(F32) > score, below, near)
            return carry

        lax.fori_loop(0, blocks_needed(seg), score_block, 0)

    def fold_rows(a, rows):
        parts = [a[rows * r:rows * (r + 1)] for r in range(a.shape[0] // rows)]
        while len(parts) > 1:
            parts = [parts[i] + parts[i + 1] for i in range(0, len(parts) - 1, 2)] + parts[len(parts) & ~1:]
        return parts[0]

    def count_rows(scr, rows, one, hit_fn):
        cnt = jnp.zeros((rows, tq), one.dtype)
        for seg in segs:
            blk, row0 = seg[1], seg[2]

            def body(j, cnt, seg=seg, blk=blk, row0=row0):
                v = scr[rows_of(seg, j), :]
                sub = 16 * rows
                for c in range(0, blk, sub):
                    hit = hit_fn(v[c:c + sub], row0 + j * blk + c)
                    cnt = cnt + fold_rows(jnp.where(hit, one, jnp.zeros_like(one)), rows)
                return cnt

            cnt = lax.fori_loop(0, blocks_needed(seg), body, cnt)
        return jnp.sum(cnt.astype(I32), axis=0, keepdims=True)

    def key_to_f32(key):
        return pltpu.bitcast(jnp.where(key < 0, key ^ jnp.int32(0x7FFFFFFF), key), F32)

    def coarse_bits(key16):
        return jnp.where(key16 < 0, key16 ^ jnp.int32(0x7FFF), key16)

    def largest_passing(n_all, count_ge):
        def step(i, carry):
            ans, n_ans = carry
            cand = jnp.where(i == 0, jnp.zeros_like(ans), ans | jnp.left_shift(jnp.int32(1), 15 - i))
            n = count_ge(cand)
            take = n >= topk
            return jnp.where(take, cand, ans), jnp.where(take, n, n_ans)

        return lax.fori_loop(0, 16, step, (jnp.full((1, tq), -32768, I32), n_all))

    total = jnp.zeros((1, tq), I32)
    for seg in segs:
        total = total + blocks_needed(seg) * seg[1]

    one16 = jnp.ones((1, 1), I16)
    one32 = jnp.ones((1, 1), I32)

    def count_coarse_ge(key16):
        cand = pltpu.bitcast(coarse_bits(key16).astype(I16), COARSE)
        return count_rows(half_scr, HALF_ROWS, one16, lambda v, r0: v >= cand)

    key16, n_coarse = largest_passing(total, count_coarse_ge)
    short = key16 == -32768
    base = lax.shift_left(coarse_bits(key16), 16)
    base = jnp.where(base < 0, base ^ jnp.int32(0x7FFFFFFF), base)

    def count_fine_ge(off):
        cand = key_to_f32(base + off + 32768)
        return count_rows(sc_scr, SUBLANES, one32, lambda v, r0: v >= cand)

    off, n_ge = largest_passing(n_coarse, count_fine_ge)
    thr = jnp.where(short, -jnp.inf, key_to_f32(jnp.where(short, 0, base + off + 32768)))

    need = live & (n_ge > topk) & jnp.logical_not(short)

    @pl.when(jnp.max(jnp.where(need, 1, 0)) > 0)
    def _():
        def count_scores(hit_fn):
            def with_rows(v, r0):
                return hit_fn(v, r0 + lax.broadcasted_iota(I32, v.shape, 0))

            return count_rows(sc_scr, SUBLANES, one32, with_rows)

        quota = topk - count_scores(lambda v, ridx: v > thr)
        nbits = int(sc_scr.shape[0]).bit_length()

        def index_bisect(i, last):
            cand = last | jnp.left_shift(jnp.int32(1), nbits - 1 - i)
            below = count_scores(lambda v, ridx: (v == thr) & (ridx < cand))
            return jnp.where(below < quota, cand, last)

        last = lax.fori_loop(0, nbits, index_bisect, jnp.zeros((1, tq), I32))

        for seg in segs:
            blk, row0 = seg[1], seg[2]

            def drop(j, carry, seg=seg, blk=blk, row0=row0):
                v = sc_scr[rows_of(seg, j), :]
                ridx = row0 + j * blk + lax.broadcasted_iota(I32, (blk, tq), 0)
                cut = need & (v == thr) & (ridx > last)
                sc_scr[rows_of(seg, j), :] = jnp.where(cut, -jnp.inf, v)
                return carry

            lax.fori_loop(0, blocks_needed(seg), drop, 0)

    thr_sel = jnp.maximum(thr, F32_LOWEST)

    def attend_all(body):
        for seg, (_, k_ref, v_ref) in zip(segs, seg_refs):
            blk = seg[1]

            def step(j, carry, seg=seg, k_ref=k_ref, v_ref=v_ref, blk=blk):
                bias = jnp.where(sc_scr[rows_of(seg, j), :] >= thr_sel, 0.0, NEG_BIG)
                bias = jnp.concatenate([bias, bias], axis=1)
                rows = pl.ds(pl.multiple_of(j * blk, blk), blk)
                kblk = k_ref[0, rows, :].astype(BF16)
                vblk = v_ref[0, rows, :].astype(F32)
                ones = jnp.ones((DEN_ROWS, blk), BF16)
                for hp in range(H_DSA // 2):
                    lanes = slice(LANES * hp, LANES * (hp + 1))
                    s = _dot(kblk[:, lanes], qm_scr[hp]) + bias
                    v_aug = jnp.concatenate([vblk[:, lanes].T.astype(BF16), ones], axis=0)
                    body(hp, s, v_aug)
                return carry

            lax.fori_loop(0, blocks_needed(seg), step, 0)

    @pl.when(qb == 0)
    def _():
        feat = lax.shift_right_logical(lax.broadcasted_iota(I32, (W_DSA, LANES), 0), _log2(DH_DSA))
        head_sum = jnp.where(feat == lax.broadcasted_iota(I32, (W_DSA, LANES), 1), 1.0, 0.0).astype(BF16)
        best = jnp.zeros((1, LANES), F32)
        for seg, (_, k_ref, _) in zip(segs, seg_refs):
            blk = seg[1]

            def norms(j, best, k_ref=k_ref, blk=blk):
                kb = k_ref[0, pl.ds(pl.multiple_of(j * blk, blk), blk), :].astype(F32)
                sq = _dot((kb * kb).astype(BF16), head_sum)
                return jnp.maximum(best, jnp.max(sq, axis=0, keepdims=True))

            best = lax.fori_loop(0, k_ref.shape[1] // blk, norms, best)
        knorm = jnp.sqrt(best)
        for h in range(H_DSA):
            kmax_scr[h // 2, :, tq * (h % 2):tq * (h % 2 + 1)] = jnp.broadcast_to(knorm[:, h:h + 1], (1, tq))

    for hp in range(H_DSA // 2):
        qp = qm_scr[hp].astype(F32)
        m_scr[hp] = jnp.sqrt(jnp.sum(qp * qp, axis=0, keepdims=True)) * kmax_scr[hp]

    def fast_pair(hp, s, v_aug):
        acc_scr[hp] = acc_scr[hp] + _dot(v_aug, jnp.exp2(s - m_scr[hp]).astype(BF16))

    attend_all(fast_pair)

    bad = jnp.zeros((1, 2 * tq), I32)
    for hp in range(H_DSA // 2):
        den = acc_scr[hp][LANES:LANES + 1]
        bad = bad | jnp.where((den >= SAFE_DENOM_MIN) & (den <= SAFE_DENOM_MAX), 0, 1)

    @pl.when(jnp.max(bad) > 0)
    def _():
        m_scr[...] = jnp.full(m_scr.shape, NEG_BIG, F32)
        acc_scr[...] = jnp.zeros(acc_scr.shape, F32)

        def online_pair(hp, s, v_aug):
            m_old = m_scr[hp]
            m_new = jnp.maximum(m_old, jnp.max(s, axis=0, keepdims=True))
            acc_scr[hp] = (jnp.exp2(m_old - m_new) * acc_scr[hp]
                           + _dot(v_aug, jnp.exp2(s - m_new).astype(BF16)))
            m_scr[hp] = m_new

        attend_all(online_pair)

    for hp in range(H_DSA // 2):
        acc = acc_scr[hp]
        den = acc[LANES:LANES + 1]
        pair_t = jnp.concatenate([acc[:DH_DSA, :tq] / den[:, :tq], acc[DH_DSA:LANES, tq:] / den[:, tq:]], axis=0)
        o_ref[:, LANES * hp:LANES * (hp + 1)] = pair_t.T.astype(BF16)


def _dsa(iq, dq, iw, key_sets, batch, t, past, tq, tq_live):
    nq = t // tq
    ltot = sum(ks[0] for ks in key_sets)
    topk = min(TOPK_MAX, ltot // 4)
    segs = []
    row = 0
    pos = 0
    for valid, kidx, _, _ in key_sets:
        length = kidx.shape[1]
        blk = min(512, length)
        assert blk % LANES == 0 and length % blk == 0 and valid <= length
        segs.append((valid, blk, row, pos))
        row += length
        pos += valid
    assert topk <= row and tq % LANES == 0
    qrow = lambda b, i: (b * nq + i, 0)
    per_b = lambda b, i: (b, 0, 0)
    in_specs = [pl.BlockSpec((tq, 512), qrow), pl.BlockSpec((tq, 512), qrow), pl.BlockSpec((tq, H_IDX), qrow)]
    operands = [iq, dq, iw]
    for _, kidx, k, v in key_sets:
        length = kidx.shape[1]
        in_specs += [pl.BlockSpec((1, length, D_IDX), per_b), pl.BlockSpec((1, length, W_DSA), per_b),
                     pl.BlockSpec((1, length, W_DSA), per_b)]
        operands += [kidx, k, v]
    return pl.pallas_call(
        functools.partial(_dsa_kernel, segs=tuple(segs), tq=tq, tq_live=tq_live, topk=topk, q_pos0=past),
        grid=(batch, nq),
        in_specs=in_specs,
        out_specs=pl.BlockSpec((tq, W_DSA), qrow),
        out_shape=jax.ShapeDtypeStruct((batch * t, W_DSA), BF16),
        scratch_shapes=[pltpu.VMEM((row, tq), F32),
                        pltpu.VMEM((row, tq), COARSE),
                        pltpu.VMEM((H_IDX, D_IDX, tq), BF16),
                        pltpu.VMEM((H_DSA // 2, LANES, 2 * tq), BF16),
                        pltpu.VMEM((H_IDX, tq), F32),
                        pltpu.VMEM((H_DSA // 2, 1, 2 * tq), F32),
                        pltpu.VMEM((H_DSA // 2, LANES + DEN_ROWS, 2 * tq), F32),
                        pltpu.VMEM((H_DSA // 2, 1, 2 * tq), F32)],
        compiler_params=pltpu.CompilerParams(dimension_semantics=("arbitrary", "arbitrary"),
                                             vmem_limit_bytes=VMEM_LIMIT_BYTES),
        name="dsa",
    )(*operands)


def _ffn_kernel(x_ref, yr_ref, od_ref, wo_ref, g2_ref, wup_ref, wg_ref, wd_ref, cw_ref, cb_ref, cs_ref, gf_ref,
                y_ref, cnew_ref, carry_scr, *, fc):
    t = pl.program_id(1)
    tm = x_ref.shape[0]
    d_ff = wup_ref.shape[1]
    nseq = carry_scr.shape[0]
    seq = tm // nseq

    @pl.when(t == 0)
    def _():
        carry_scr[...] = cs_ref[...]

    x1 = x_ref[...] + _dot(yr_ref[...], wo_ref[:W_RET, :]) + _dot(od_ref[...], wo_ref[W_RET:, :])
    h2 = (_rms(x1) * g2_ref[...]).astype(BF16)
    row = lax.broadcasted_iota(I32, (tm, fc), 0) & (seq - 1)
    acc = jnp.zeros(x1.shape, F32)

    def carried(r, cols):
        if nseq == 1:
            return carry_scr[0, r:r + 1, cols]
        return jnp.concatenate([jnp.broadcast_to(carry_scr[s, r:r + 1, cols], (seq, fc)) for s in range(nseq)], axis=0)

    for c0 in range(0, d_ff, fc):
        cols = slice(c0, c0 + fc)
        a = _dot(h2, wup_ref[:, cols])
        u = _dot(h2, wg_ref[:, cols])
        prev2, prev1 = carried(0, cols), carried(1, cols)
        a1 = jnp.where(row == 0, prev1, pltpu.roll(a, 1, 0))
        a2 = jnp.where(row == 0, prev2, jnp.where(row == 1, prev1, pltpu.roll(a, 2, 0)))
        cw = cw_ref[:, cols]
        conv = cb_ref[:, cols] + a2 * cw[0:1] + a1 * cw[1:2] + a * cw[2:3]
        for s in range(nseq):
            carry_scr[s, :, cols] = a[(s + 1) * seq - 2:(s + 1) * seq, :]
        acc = acc + _dot((_silu(conv) * u).astype(BF16), wd_ref[cols, :])
    y_ref[...] = _rms(x1 + acc) * gf_ref[...]

    @pl.when(t == pl.num_programs(1) - 1)
    def _():
        cnew_ref[...] = carry_scr[...]


def _out_ffn(x2d, y_ret, o_dsa, w_out, g2, w_up, w_gate, w_down, conv_w, conv_b, conv_state, g_final,
             batch, t, tm):
    n, d = x2d.shape
    d_ff = w_up.shape[1]
    nseq = max(1, tm // t)
    assert tm % nseq == 0 and (tm // nseq) & (tm // nseq - 1) == 0 and batch % nseq == 0 and t % (tm // nseq) == 0
    nt = t * nseq // tm
    fc = 256
    assert d_ff % fc == 0
    row = lambda g, i: (g * nt + i, 0)
    const = lambda g, i: (0, 0)
    per_g = lambda g, i: (g, 0, 0)
    return pl.pallas_call(
        functools.partial(_ffn_kernel, fc=fc),
        grid=(batch // nseq, nt),
        in_specs=[pl.BlockSpec((tm, d), row), pl.BlockSpec((tm, W_RET), row), pl.BlockSpec((tm, W_DSA), row),
                  pl.BlockSpec(w_out.shape, const), pl.BlockSpec((1, d), const),
                  pl.BlockSpec(w_up.shape, const), pl.BlockSpec(w_gate.shape, const),
                  pl.BlockSpec(w_down.shape, const), pl.BlockSpec(conv_w.shape, const),
                  pl.BlockSpec((1, d_ff), const), pl.BlockSpec((nseq, CONV_W - 1, d_ff), per_g),
                  pl.BlockSpec((1, d), const)],
        out_specs=(pl.BlockSpec((tm, d), row), pl.BlockSpec((nseq, CONV_W - 1, d_ff), per_g)),
        out_shape=(jax.ShapeDtypeStruct((n, d), F32), jax.ShapeDtypeStruct((batch, CONV_W - 1, d_ff), F32)),
        scratch_shapes=[pltpu.VMEM((nseq, CONV_W - 1, d_ff), F32)],
        compiler_params=pltpu.CompilerParams(dimension_semantics=("arbitrary", "arbitrary"),
                                             vmem_limit_bytes=VMEM_LIMIT_BYTES),
        name="out_ffn",
    )(x2d, y_ret, o_dsa, w_out, g2, w_up, w_gate, w_down, conv_w, conv_b, conv_state, g_final)


def _rope_tables(past, t, rows):
    half = DK_RET // 2
    inv_freq = ROPE_BASE ** (-jnp.arange(half, dtype=F32) / half)
    pos = past + jnp.arange(t, dtype=jnp.int32)
    ang = pos.astype(F32)[:, None] * inv_freq[None, :]
    cos, sin = jnp.cos(ang), jnp.sin(ang)
    cos_t = jnp.concatenate([cos, cos, cos, cos], axis=1)
    sin_t = jnp.concatenate([-sin, sin, -sin, sin], axis=1)
    if t < rows:
        cos_t = jnp.tile(cos_t, (rows // t, 1))
        sin_t = jnp.tile(sin_t, (rows // t, 1))
    return cos_t, sin_t


def _trunk_layer(x, ret_state, past_kv, conv_state, weights, g_final):
    w_in_pad, w_out, g_mix, g_gn, g_ffn, w_up, w_gate, w_down, conv_w, conv_b = weights
    batch, t, d = x.shape
    n = batch * t
    past = 0 if past_kv is None else past_kv[0].shape[1]
    x2d = x.reshape(n, d)

    tm_in = min(512, n)
    cos_t, sin_t = _rope_tables(past, t, tm_in)
    (rq, rk, rv, rg, dq, dk, dv, dkb, dvb, iq, ik, ikb, iw) = _in_proj(x2d, g_mix, w_in_pad, cos_t, sin_t, tm_in)

    y_ret, ret_new = _retention(rq, rk, rv, rg, ret_state, g_gn, batch, t)

    if past_kv is None:
        key_sets = [(t, ikb.reshape(batch, t, D_IDX), dkb.reshape(batch, t, W_DSA), dvb.reshape(batch, t, W_DSA))]
    else:
        past_k, past_v, past_ki = past_kv
        rows = -(-t // LANES) * LANES
        padded = lambda a, w: jnp.pad(a.reshape(batch, t, w), ((0, 0), (0, rows - t), (0, 0)))
        flat = lambda a: a.astype(BF16).reshape(batch, past, W_DSA)
        key_sets = [(past, past_ki, flat(past_k), flat(past_v)),
                    (t, padded(ikb, D_IDX), padded(dkb, W_DSA), padded(dvb, W_DSA))]
    tq = min(256, -(-t // LANES) * LANES)
    if t % tq:
        qpad = lambda a: jnp.pad(a.reshape(batch, t, -1), ((0, 0), (0, tq - t), (0, 0))).reshape(batch * tq, -1)
        o_dsa = _dsa(qpad(iq), qpad(dq), qpad(iw), key_sets, batch, tq, past, tq, t)
        o_dsa = o_dsa.reshape(batch, tq, W_DSA)[:, :t].reshape(n, W_DSA)
    else:
        o_dsa = _dsa(iq, dq, iw, key_sets, batch, t, past, tq, tq)

    y, conv_new = _out_ffn(x2d, y_ret, o_dsa, w_out, g_ffn, w_up, w_gate, w_down, conv_w, conv_b,
                           conv_state, g_final, batch, t, min(512, n))
    return (y.reshape(batch, t, d), dk.reshape(batch, t, H_DSA, DH_DSA), dv.reshape(batch, t, H_DSA, DH_DSA),
            ik.reshape(batch, t, D_IDX), ret_new, conv_new)


def kernel(x_prompt, x_sample, cache_dsa_k, cache_dsa_v, cache_idx_k, state_ret, state_ffn_conv, w_in, w_out,
           g_norm_mix, g_gn_ret, g_norm_ffn, w_up, w_gate, w_down, conv_w, conv_b, g_norm_final):
    depth = w_in.shape[0]
    assert depth == 1, "the final norm is fused into the layer's last kernel"
    bp = x_prompt.shape[0]
    d_ff = w_up.shape[-1]
    l = 0
    p_in = w_in.shape[-1]
    pad = (-p_in) % LANES
    weights = (jnp.pad(w_in[l], ((0, 0), (0, pad))).astype(BF16), w_out[l].astype(BF16),
               g_norm_mix[l][None, :], g_gn_ret[l][None, :], g_norm_ffn[l][None, :],
               w_up[l].astype(BF16), w_gate[l].astype(BF16), w_down[l].astype(BF16),
               conv_w[l], conv_b[l][None, :])
    g_final = g_norm_final[None, :]

    yp, pk, pv, pki, pret, pconv = _trunk_layer(
        x_prompt, jnp.zeros((bp, H_RET, DK_RET, DV_RET), F32), None,
        jnp.zeros((bp, CONV_W - 1, d_ff), F32), weights, g_final)
    ys, sk, sv, ski, sret, sconv = _trunk_layer(
        x_sample, state_ret[l], (cache_dsa_k[l], cache_dsa_v[l], cache_idx_k[l]),
        state_ffn_conv[l], weights, g_final)
    st = lambda a: a[None]
    return (yp, ys, st(pk), st(pv), st(pki), st(pret), st(pconv),
            st(sk), st(sv), st(ski), st(sret), st(sconv))
```

```python
import functools

import numpy as np
import jax
import jax.numpy as jnp
from jax import lax
from jax.experimental import pallas as pl
from jax.experimental.pallas import tpu as pltpu

F32 = jnp.float32
BF16 = jnp.bfloat16
I32 = jnp.int32
I16 = jnp.int16
COARSE = jnp.bfloat16

CHUNK = 64
H_RET = 8
DK_RET = 64
DV_RET = 64
H_DSA = 8
DH_DSA = 64
H_IDX = 8
D_IDX = 64
TOPK_MAX = 256
CONV_W = 3
ROPE_BASE = 10000.0
EPS = 1e-6
W_RET = H_RET * DV_RET
W_DSA = H_DSA * DH_DSA

LANES = 128
SUBLANES = 8
HALF_ROWS = 16
DEN_ROWS = 16
KEY_BLOCK = 1024
COUNT_BLOCK = 512
VMEM_LIMIT_BYTES = 56 * 1024 * 1024
INT_MIN = -(2 ** 31)
F32_LOWEST = -3.4028234663852886e38
NEG_BIG = -1e30
LOG2E = 1.4426950408889634
SAFE_DENOM_MIN = 2.0 ** -100
SAFE_DENOM_MAX = 2.0 ** 100


def _dot(a, b):
    return jnp.dot(a, b, preferred_element_type=F32)


def _dot_nt(a, b):
    return lax.dot_general(a, b, (((1,), (1,)), ((), ())), preferred_element_type=F32)


def _dot_tn(a, b):
    return lax.dot_general(a, b, (((0,), (0,)), ((), ())), preferred_element_type=F32)


def _rms(x):
    return x * lax.rsqrt(jnp.mean(x * x, axis=-1, keepdims=True) + EPS)


def _silu(x):
    return x / (1.0 + jnp.exp(-x))


def _inproj_kernel(x_ref, g_ref, w_ref, cos_ref, sin_ref,
                   rq_ref, rk_ref, rv_ref, rg_ref, dq_ref, dk_ref, dv_ref, dkb_ref, dvb_ref,
                   iq_ref, ik_ref, ikb_ref, iw_ref):
    tm = x_ref.shape[0]
    h = (_rms(x_ref[...]) * g_ref[...]).astype(BF16)

    def proj(c0, width):
        return _dot(h, w_ref[:, c0:c0 + width])

    cos = jnp.concatenate([cos_ref[...]] * (W_RET // LANES), axis=1)
    sin = jnp.concatenate([sin_ref[...]] * (W_RET // LANES), axis=1)
    lane = lax.broadcasted_iota(I32, (tm, W_RET), 1)
    first_half = (lane & (DK_RET - 1)) < (DK_RET // 2)

    def rope(p):
        swapped = jnp.where(first_half, pltpu.roll(p, W_RET - DK_RET // 2, 1), pltpu.roll(p, DK_RET // 2, 1))
        return p * cos + swapped * sin

    rq_ref[...] = rope(proj(0, 512)).astype(BF16)
    rk_ref[...] = rope(proj(512, 512)) * (DK_RET ** -0.5)
    rv_ref[...] = proj(1024, 512).astype(BF16)
    rg_ref[...] = proj(1536, 512)
    dq_ref[...] = (proj(2048, 512) * (DH_DSA ** -0.5 * LOG2E)).astype(BF16)
    dk = proj(2560, 512)
    dk_ref[...] = dk
    dkb_ref[...] = dk.astype(BF16)
    dv = proj(3072, 512)
    dv_ref[...] = dv
    dvb_ref[...] = dv.astype(BF16)
    iq_ref[...] = (proj(3584, 512) * (D_IDX ** -0.5)).astype(BF16)
    tail = proj(4096, LANES)
    ik = tail[:, :D_IDX]
    ik_ref[...] = ik
    ikb_ref[...] = ik.astype(BF16)
    iw_ref[...] = tail[:, D_IDX:D_IDX + H_IDX] * (H_IDX ** -0.5)


def _in_proj(x2d, g, w_pad, cos_t, sin_t, tm):
    n, d = x2d.shape
    nt = n // tm
    ntab = cos_t.shape[0] // tm
    row = lambda i: (i, 0)
    const = lambda i: (0, 0)
    tab = lambda i: (i % ntab, 0)
    wide = lambda dt: jax.ShapeDtypeStruct((n, 512), dt)
    out_shape = (wide(BF16), wide(F32), wide(BF16), wide(F32), wide(BF16), wide(F32), wide(F32),
                 wide(BF16), wide(BF16), wide(BF16),
                 jax.ShapeDtypeStruct((n, D_IDX), F32), jax.ShapeDtypeStruct((n, D_IDX), BF16),
                 jax.ShapeDtypeStruct((n, H_IDX), F32))
    out_specs = tuple([pl.BlockSpec((tm, 512), row)] * 10
                      + [pl.BlockSpec((tm, D_IDX), row), pl.BlockSpec((tm, D_IDX), row),
                         pl.BlockSpec((tm, H_IDX), row)])
    return pl.pallas_call(
        _inproj_kernel,
        grid=(nt,),
        in_specs=[pl.BlockSpec((tm, d), row), pl.BlockSpec((1, d), const),
                  pl.BlockSpec(w_pad.shape, const),
                  pl.BlockSpec((tm, LANES), tab), pl.BlockSpec((tm, LANES), tab)],
        out_specs=out_specs,
        out_shape=out_shape,
        compiler_params=pltpu.CompilerParams(dimension_semantics=("arbitrary",),
                                             vmem_limit_bytes=VMEM_LIMIT_BYTES),
        name="in_proj",
    )(x2d, g, w_pad, cos_t, sin_t)


def _ret_kernel(q_ref, k_ref, v_ref, g_ref, s0_ref, dec_ref, xi_ref, zeta_ref, gn_ref,
                y_ref, sout_ref, s_scr, *, gamma_c):
    c = pl.program_id(1)
    chunk = q_ref.shape[0]
    npair = H_RET // 2
    blk_row = lax.broadcasted_iota(I32, (LANES, LANES), 0) < DK_RET
    blk_col = lax.broadcasted_iota(I32, (LANES, LANES), 1) < DV_RET
    own_block = blk_row == blk_col
    low = lax.broadcasted_iota(I32, (chunk, LANES), 1) < DV_RET

    @pl.when(c == 0)
    def _():
        for p in range(npair):
            s_scr[p] = jnp.zeros((LANES, LANES), F32)
            s_scr[p, :DK_RET, :DV_RET] = s0_ref[0, 2 * p]
            s_scr[p, DK_RET:, DV_RET:] = s0_ref[0, 2 * p + 1]

    q = q_ref[...]
    k = k_ref[...]
    v = v_ref[...]
    kb = k.astype(BF16)
    kz = (k * zeta_ref[...]).astype(BF16)
    xi = xi_ref[...]
    normed = []
    for p in range(npair):
        sl = slice(LANES * p, LANES * (p + 1))
        qp, kp, vp, kzp = q[:, sl], kb[:, sl], v[:, sl], kz[:, sl]
        state = s_scr[p]
        q2 = jnp.concatenate([jnp.where(low, qp, jnp.zeros_like(qp)), jnp.where(low, jnp.zeros_like(qp), qp)], axis=0)
        scores = _dot_nt(q2, kp) * dec_ref[p]
        both = _dot(scores.astype(BF16), vp)
        inner = jnp.where(low, both[:chunk], both[chunk:])
        cross = _dot(qp, state.astype(BF16)) * xi[:, sl]
        o = inner + cross
        decay_rows = jnp.where(blk_row, gamma_c[2 * p], gamma_c[2 * p + 1])
        s_scr[p] = decay_rows * state + jnp.where(own_block, _dot_tn(kzp, vp), 0.0)
        inv = 1.0 / DV_RET
        mu = jnp.where(low, jnp.sum(jnp.where(low, o, 0.0), axis=-1, keepdims=True),
                       jnp.sum(jnp.where(low, 0.0, o), axis=-1, keepdims=True)) * inv
        dev = o - mu
        sq = dev * dev
        var = jnp.where(low, jnp.sum(jnp.where(low, sq, 0.0), axis=-1, keepdims=True),
                        jnp.sum(jnp.where(low, 0.0, sq), axis=-1, keepdims=True)) * inv
        normed.append(dev * lax.rsqrt(var + EPS))
    on = jnp.concatenate(normed, axis=1) * gn_ref[...]
    y_ref[...] = (_silu(g_ref[...]) * on).astype(BF16)

    @pl.when(c == pl.num_programs(1) - 1)
    def _():
        for p in range(npair):
            sout_ref[0, 2 * p] = s_scr[p, :DK_RET, :DV_RET]
            sout_ref[0, 2 * p + 1] = s_scr[p, DK_RET:, DV_RET:]


def _retention_constants(c):
    lg = np.log1p(-np.exp2(-5.0 - np.arange(H_RET, dtype=np.float64)))
    idx = np.arange(c, dtype=np.float64)
    diff = idx[:, None] - idx[None, :]
    decay = np.where(diff[None] >= 0, np.exp(lg[:, None, None] * np.maximum(diff, 0.0)[None]), 0.0)
    xi = np.exp(lg[None, :] * (idx + 1.0)[:, None])
    zeta = np.exp(lg[None, :] * (c - 1.0 - idx)[:, None])
    gamma_c = tuple(float(np.float32(g)) for g in np.exp(lg * c))
    rep = lambda a: np.repeat(a, DK_RET, axis=1).astype(np.float32)
    pair_decay = decay.reshape(H_RET // 2, 2 * c, c)
    return pair_decay.astype(np.float32), rep(xi), rep(zeta), gamma_c


def _retention(rq, rk, rv, rg, s0, gn, batch, t):
    c = min(4 * CHUNK, t)
    nc = t // c
    decay, xi, zeta, gamma_c = _retention_constants(c)
    blk = lambda b, j: (b * nc + j, 0)
    const2 = lambda b, j: (0, 0)
    const3 = lambda b, j: (0, 0, 0)
    st = lambda b, j: (b, 0, 0, 0)
    n = batch * t
    return pl.pallas_call(
        functools.partial(_ret_kernel, gamma_c=gamma_c),
        grid=(batch, nc),
        in_specs=[pl.BlockSpec((c, W_RET), blk)] * 4
                 + [pl.BlockSpec((1, H_RET, DK_RET, DV_RET), st),
                    pl.BlockSpec((H_RET // 2, 2 * c, c), const3),
                    pl.BlockSpec((c, W_RET), const2), pl.BlockSpec((c, W_RET), const2),
                    pl.BlockSpec((1, W_RET), const2)],
        out_specs=(pl.BlockSpec((c, W_RET), blk), pl.BlockSpec((1, H_RET, DK_RET, DV_RET), st)),
        out_shape=(jax.ShapeDtypeStruct((n, W_RET), BF16),
                   jax.ShapeDtypeStruct((batch, H_RET, DK_RET, DV_RET), F32)),
        scratch_shapes=[pltpu.VMEM((H_RET // 2, LANES, LANES), F32)],
        compiler_params=pltpu.CompilerParams(dimension_semantics=("arbitrary", "arbitrary"),
                                             vmem_limit_bytes=VMEM_LIMIT_BYTES),
        name="retention",
    )(rq, rk, rv, rg, s0, jnp.asarray(decay), jnp.asarray(xi), jnp.asarray(zeta), gn)


def _log2(n):
    assert n > 0 and n & (n - 1) == 0
    return n.bit_length() - 1


def _dsa_kernel(*refs, segs, tq, tq_live, topk, q_pos0):
    nseg = len(segs)
    iq_ref, dq_ref, iw_ref = refs[:3]
    seg_refs = [refs[3 + 3 * s: 6 + 3 * s] for s in range(nseg)]
    o_ref = refs[3 + 3 * nseg]
    sc_scr, half_scr, qi_scr, qm_scr, w_scr, m_scr, acc_scr, kmax_scr = refs[4 + 3 * nseg:]

    chunk_shift = _log2(CHUNK)
    qb = pl.program_id(1)
    q_first = q_pos0 + qb * tq
    q_last_chunk = lax.shift_right_logical(q_first + tq - 1, chunk_shift)
    q_chunk = lax.shift_right_logical(q_first + lax.broadcasted_iota(I32, (1, tq), 1), chunk_shift)
    live = lax.broadcasted_iota(I32, (1, tq), 1) < tq_live

    iq_t = iq_ref[...].astype(F32).T
    dq_t = dq_ref[...].astype(F32).T
    w_scr[...] = iw_ref[...].T
    pair_row = lax.broadcasted_iota(I32, (LANES, tq), 0)
    for h in range(H_IDX):
        qi_scr[h] = iq_t[D_IDX * h:D_IDX * (h + 1)].astype(BF16)
    for hp in range(H_DSA // 2):
        pair = dq_t[LANES * hp:LANES * (hp + 1)]
        qm_scr[hp] = jnp.concatenate([jnp.where(pair_row < DH_DSA, pair, 0.0),
                                      jnp.where(pair_row >= DH_DSA, pair, 0.0)], axis=1).astype(BF16)
    acc_scr[...] = jnp.zeros(acc_scr.shape, F32)

    def blocks_needed(seg):
        valid, blk, _, pos0 = seg
        adm = jnp.clip((q_last_chunk + 1) * CHUNK - pos0, 0, valid)
        return lax.shift_right_logical(adm + blk - 1, _log2(blk))

    def rows_of(seg, j):
        _, blk, row0, _ = seg
        return pl.ds(pl.multiple_of(row0 + j * blk, blk), blk)

    for seg, (kidx_ref, _, _) in zip(segs, seg_refs):
        valid, blk, _, pos0 = seg

        def score_block(j, carry, seg=seg, kidx_ref=kidx_ref, valid=valid, blk=blk, pos0=pos0):
            kidx = kidx_ref[0, pl.ds(pl.multiple_of(j * blk, blk), blk), :].astype(BF16)
            tot = jnp.zeros((blk, tq), F32)
            for h in range(H_IDX):
                tot = tot + jnp.maximum(_dot(kidx, qi_scr[h]), 0.0) * w_scr[h:h + 1, :]
            idx = j * blk + lax.broadcasted_iota(I32, (blk, tq), 0)
            ok = (lax.shift_right_logical(pos0 + idx, chunk_shift) <= q_chunk) & (idx < valid)
            score = jnp.where(ok, tot, -jnp.inf)
            sc_scr[rows_of(seg, j), :] = score
            near = score.astype(COARSE)
            bits = pltpu.bitcast(near, I16)
            below = pltpu.bitcast(bits + jnp.where(bits < 0, jnp.int16(1), jnp.int16(-1)), COARSE)
            half_scr[rows_of(seg, j), :] = jnp.where(near.astype(F32) > score, below, near)
            return carry

        lax.fori_loop(0, blocks_needed(seg), score_block, 0)

    def fold_rows(a, rows):
        parts = [a[rows * r:rows * (r + 1)] for r in range(a.shape[0] // rows)]
        while len(parts) > 1:
            parts = [parts[i] + parts[i + 1] for i in range(0, len(parts) - 1, 2)] + parts[len(parts) & ~1:]
        return parts[0]

    def count_rows(scr, rows, one, hit_fn):
        cnt = jnp.zeros((rows, tq), one.dtype)
        for seg in segs:
            seg = (seg[0], min(COUNT_BLOCK, seg[1]), seg[2], seg[3])
            blk, row0 = seg[1], seg[2]

            def body(j, cnt, seg=seg, blk=blk, row0=row0):
                v = scr[rows_of(seg, j), :]
                sub = 16 * rows
                for c in range(0, blk, sub):
                    hit = hit_fn(v[c:c + sub], row0 + j * blk + c)
                    cnt = cnt + fold_rows(jnp.where(hit, one, jnp.zeros_like(one)), rows)
                return cnt

            cnt = lax.fori_loop(0, blocks_needed(seg), body, cnt)
        return jnp.sum(cnt.astype(I32), axis=0, keepdims=True)

    def key_to_f32(key):
        return pltpu.bitcast(jnp.where(key < 0, key ^ jnp.int32(0x7FFFFFFF), key), F32)

    def coarse_bits(key16):
        return jnp.where(key16 < 0, key16 ^ jnp.int32(0x7FFF), key16)

    def largest_passing(n_all, count_ge):
        def step(i, carry):
            ans, n_ans = carry
            cand = jnp.where(i == 0, jnp.zeros_like(ans), ans | jnp.left_shift(jnp.int32(1), 15 - i))
            n = count_ge(cand)
            take = n >= topk
            return jnp.where(take, cand, ans), jnp.where(take, n, n_ans)

        return lax.fori_loop(0, 16, step, (jnp.full((1, tq), -32768, I32), n_all))

    total = jnp.zeros((1, tq), I32)
    for seg in segs:
        total = total + blocks_needed(seg) * seg[1]

    one16 = jnp.ones((1, 1), I16)
    one32 = jnp.ones((1, 1), I32)

    def count_coarse_ge(key16):
        cand = pltpu.bitcast(coarse_bits(key16).astype(I16), COARSE)
        return count_rows(half_scr, HALF_ROWS, one16, lambda v, r0: v >= cand)

    key16, n_coarse = largest_passing(total, count_coarse_ge)
    short = key16 == -32768
    base = lax.shift_left(coarse_bits(key16), 16)
    base = jnp.where(base < 0, base ^ jnp.int32(0x7FFFFFFF), base)

    def count_fine_ge(off):
        cand = key_to_f32(base + off + 32768)
        return count_rows(sc_scr, SUBLANES, one32, lambda v, r0: v >= cand)

    off, n_ge = largest_passing(n_coarse, count_fine_ge)
    thr = jnp.where(short, -jnp.inf, key_to_f32(jnp.where(short, 0, base + off + 32768)))

    need = live & (n_ge > topk) & jnp.logical_not(short)

    @pl.when(jnp.max(jnp.where(need, 1, 0)) > 0)
    def _():
        def count_scores(hit_fn):
            def with_rows(v, r0):
                return hit_fn(v, r0 + lax.broadcasted_iota(I32, v.shape, 0))

            return count_rows(sc_scr, SUBLANES, one32, with_rows)

        quota = topk - count_scores(lambda v, ridx: v > thr)
        nbits = int(sc_scr.shape[0]).bit_length()

        def index_bisect(i, last):
            cand = last | jnp.left_shift(jnp.int32(1), nbits - 1 - i)
            below = count_scores(lambda v, ridx: (v == thr) & (ridx < cand))
            return jnp.where(below < quota, cand, last)

        last = lax.fori_loop(0, nbits, index_bisect, jnp.zeros((1, tq), I32))

        for seg in segs:
            blk, row0 = seg[1], seg[2]

            def drop(j, carry, seg=seg, blk=blk, row0=row0):
                v = sc_scr[rows_of(seg, j), :]
                ridx = row0 + j * blk + lax.broadcasted_iota(I32, (blk, tq), 0)
                cut = need & (v == thr) & (ridx > last)
                sc_scr[rows_of(seg, j), :] = jnp.where(cut, -jnp.inf, v)
                return carry

            lax.fori_loop(0, blocks_needed(seg), drop, 0)

    thr_sel = jnp.maximum(thr, F32_LOWEST)

    def attend_all(body):
        for seg, (_, k_ref, v_ref) in zip(segs, seg_refs):
            blk = seg[1]

            def step(j, carry, seg=seg, k_ref=k_ref, v_ref=v_ref, blk=blk):
                bias = jnp.where(sc_scr[rows_of(seg, j), :] >= thr_sel, 0.0, NEG_BIG)
                bias = jnp.concatenate([bias, bias], axis=1)
                rows = pl.ds(pl.multiple_of(j * blk, blk), blk)
                kblk = k_ref[0, rows, :].astype(BF16)
                vblk = v_ref[0, rows, :].astype(F32)
                ones = jnp.ones((DEN_ROWS, blk), BF16)
                for hp in range(H_DSA // 2):
                    lanes = slice(LANES * hp, LANES * (hp + 1))
                    s = _dot(kblk[:, lanes], qm_scr[hp]) + bias
                    v_aug = jnp.concatenate([vblk[:, lanes].T.astype(BF16), ones], axis=0)
                    body(hp, s, v_aug)
                return carry

            lax.fori_loop(0, blocks_needed(seg), step, 0)

    @pl.when(qb == 0)
    def _():
        feat = lax.shift_right_logical(lax.broadcasted_iota(I32, (W_DSA, LANES), 0), _log2(DH_DSA))
        head_sum = jnp.where(feat == lax.broadcasted_iota(I32, (W_DSA, LANES), 1), 1.0, 0.0).astype(BF16)
        best = jnp.zeros((1, LANES), F32)
        for seg, (_, k_ref, _) in zip(segs, seg_refs):
            blk = seg[1]

            def norms(j, best, k_ref=k_ref, blk=blk):
                kb = k_ref[0, pl.ds(pl.multiple_of(j * blk, blk), blk), :].astype(F32)
                sq = _dot((kb * kb).astype(BF16), head_sum)
                return jnp.maximum(best, jnp.max(sq, axis=0, keepdims=True))

            best = lax.fori_loop(0, k_ref.shape[1] // blk, norms, best)
        knorm = jnp.sqrt(best)
        for h in range(H_DSA):
            kmax_scr[h // 2, :, tq * (h % 2):tq * (h % 2 + 1)] = jnp.broadcast_to(knorm[:, h:h + 1], (1, tq))

    for hp in range(H_DSA // 2):
        qp = qm_scr[hp].astype(F32)
        m_scr[hp] = jnp.sqrt(jnp.sum(qp * qp, axis=0, keepdims=True)) * kmax_scr[hp]

    def fast_pair(hp, s, v_aug):
        acc_scr[hp] = acc_scr[hp] + _dot(v_aug, jnp.exp2(s - m_scr[hp]).astype(BF16))

    attend_all(fast_pair)

    bad = jnp.zeros((1, 2 * tq), I32)
    for hp in range(H_DSA // 2):
        den = acc_scr[hp][LANES:LANES + 1]
        bad = bad | jnp.where((den >= SAFE_DENOM_MIN) & (den <= SAFE_DENOM_MAX), 0, 1)

    @pl.when(jnp.max(bad) > 0)
    def _():
        m_scr[...] = jnp.full(m_scr.shape, NEG_BIG, F32)
        acc_scr[...] = jnp.zeros(acc_scr.shape, F32)

        def online_pair(hp, s, v_aug):
            m_old = m_scr[hp]
            m_new = jnp.maximum(m_old, jnp.max(s, axis=0, keepdims=True))
            acc_scr[hp] = (jnp.exp2(m_old - m_new) * acc_scr[hp]
                           + _dot(v_aug, jnp.exp2(s - m_new).astype(BF16)))
            m_scr[hp] = m_new

        attend_all(online_pair)

    for hp in range(H_DSA // 2):
        acc = acc_scr[hp]
        den = acc[LANES:LANES + 1]
        pair_t = jnp.concatenate([acc[:DH_DSA, :tq] / den[:, :tq], acc[DH_DSA:LANES, tq:] / den[:, tq:]], axis=0)
        o_ref[:, LANES * hp:LANES * (hp + 1)] = pair_t.T.astype(BF16)


def _dsa(iq, dq, iw, key_sets, batch, t, past, tq, tq_live):
    nq = t // tq
    ltot = sum(ks[0] for ks in key_sets)
    topk = min(TOPK_MAX, ltot // 4)
    segs = []
    row = 0
    pos = 0
    for valid, kidx, _, _ in key_sets:
        length = kidx.shape[1]
        blk = min(KEY_BLOCK, length)
        assert blk % LANES == 0 and length % blk == 0 and valid <= length
        segs.append((valid, blk, row, pos))
        row += length
        pos += valid
    assert topk <= row and tq % LANES == 0
    qrow = lambda b, i: (b * nq + i, 0)
    per_b = lambda b, i: (b, 0, 0)
    in_specs = [pl.BlockSpec((tq, 512), qrow), pl.BlockSpec((tq, 512), qrow), pl.BlockSpec((tq, H_IDX), qrow)]
    operands = [iq, dq, iw]
    for _, kidx, k, v in key_sets:
        length = kidx.shape[1]
        in_specs += [pl.BlockSpec((1, length, D_IDX), per_b), pl.BlockSpec((1, length, W_DSA), per_b),
                     pl.BlockSpec((1, length, W_DSA), per_b)]
        operands += [kidx, k, v]
    return pl.pallas_call(
        functools.partial(_dsa_kernel, segs=tuple(segs), tq=tq, tq_live=tq_live, topk=topk, q_pos0=past),
        grid=(batch, nq),
        in_specs=in_specs,
        out_specs=pl.BlockSpec((tq, W_DSA), qrow),
        out_shape=jax.ShapeDtypeStruct((batch * t, W_DSA), BF16),
        scratch_shapes=[pltpu.VMEM((row, tq), F32),
                        pltpu.VMEM((row, tq), COARSE),
                        pltpu.VMEM((H_IDX, D_IDX, tq), BF16),
                        pltpu.VMEM((H_DSA // 2, LANES, 2 * tq), BF16),
                        pltpu.VMEM((H_IDX, tq), F32),
                        pltpu.VMEM((H_DSA // 2, 1, 2 * tq), F32),
                        pltpu.VMEM((H_DSA // 2, LANES + DEN_ROWS, 2 * tq), F32),
                        pltpu.VMEM((H_DSA // 2, 1, 2 * tq), F32)],
        compiler_params=pltpu.CompilerParams(dimension_semantics=("arbitrary", "arbitrary"),
                                             vmem_limit_bytes=VMEM_LIMIT_BYTES),
        name="dsa",
    )(*operands)


def _ffn_kernel(x_ref, yr_ref, od_ref, wo_ref, g2_ref, wup_ref, wg_ref, wd_ref, cw_ref, cb_ref, cs_ref, gf_ref,
                y_ref, cnew_ref, carry_scr, *, fc):
    t = pl.program_id(1)
    tm = x_ref.shape[0]
    d_ff = wup_ref.shape[1]
    nseq = carry_scr.shape[0]
    seq = tm // nseq

    @pl.when(t == 0)
    def _():
        carry_scr[...] = cs_ref[...]

    x1 = x_ref[...] + _dot(yr_ref[...], wo_ref[:W_RET, :]) + _dot(od_ref[...], wo_ref[W_RET:, :])
    h2 = (_rms(x1) * g2_ref[...]).astype(BF16)
    row = lax.broadcasted_iota(I32, (tm, fc), 0) & (seq - 1)
    acc = jnp.zeros(x1.shape, F32)

    def carried(r, cols):
        if nseq == 1:
            return carry_scr[0, r:r + 1, cols]
        return jnp.concatenate([jnp.broadcast_to(carry_scr[s, r:r + 1, cols], (seq, fc)) for s in range(nseq)], axis=0)

    for c0 in range(0, d_ff, fc):
        cols = slice(c0, c0 + fc)
        a = _dot(h2, wup_ref[:, cols])
        u = _dot(h2, wg_ref[:, cols])
        prev2, prev1 = carried(0, cols), carried(1, cols)
        a1 = jnp.where(row == 0, prev1, pltpu.roll(a, 1, 0))
        a2 = jnp.where(row == 0, prev2, jnp.where(row == 1, prev1, pltpu.roll(a, 2, 0)))
        cw = cw_ref[:, cols]
        conv = cb_ref[:, cols] + a2 * cw[0:1] + a1 * cw[1:2] + a * cw[2:3]
        for s in range(nseq):
            carry_scr[s, :, cols] = a[(s + 1) * seq - 2:(s + 1) * seq, :]
        acc = acc + _dot((_silu(conv) * u).astype(BF16), wd_ref[cols, :])
    y_ref[...] = _rms(x1 + acc) * gf_ref[...]

    @pl.when(t == pl.num_programs(1) - 1)
    def _():
        cnew_ref[...] = carry_scr[...]


def _out_ffn(x2d, y_ret, o_dsa, w_out, g2, w_up, w_gate, w_down, conv_w, conv_b, conv_state, g_final,
             batch, t, tm):
    n, d = x2d.shape
    d_ff = w_up.shape[1]
    nseq = max(1, tm // t)
    assert tm % nseq == 0 and (tm // nseq) & (tm // nseq - 1) == 0 and batch % nseq == 0 and t % (tm // nseq) == 0
    nt = t * nseq // tm
    fc = d_ff
    assert d_ff % fc == 0 and fc % LANES == 0
    row = lambda g, i: (g * nt + i, 0)
    const = lambda g, i: (0, 0)
    per_g = lambda g, i: (g, 0, 0)
    return pl.pallas_call(
        functools.partial(_ffn_kernel, fc=fc),
        grid=(batch // nseq, nt),
        in_specs=[pl.BlockSpec((tm, d), row), pl.BlockSpec((tm, W_RET), row), pl.BlockSpec((tm, W_DSA), row),
                  pl.BlockSpec(w_out.shape, const), pl.BlockSpec((1, d), const),
                  pl.BlockSpec(w_up.shape, const), pl.BlockSpec(w_gate.shape, const),
                  pl.BlockSpec(w_down.shape, const), pl.BlockSpec(conv_w.shape, const),
                  pl.BlockSpec((1, d_ff), const), pl.BlockSpec((nseq, CONV_W - 1, d_ff), per_g),
                  pl.BlockSpec((1, d), const)],
        out_specs=(pl.BlockSpec((tm, d), row), pl.BlockSpec((nseq, CONV_W - 1, d_ff), per_g)),
        out_shape=(jax.ShapeDtypeStruct((n, d), F32), jax.ShapeDtypeStruct((batch, CONV_W - 1, d_ff), F32)),
        scratch_shapes=[pltpu.VMEM((nseq, CONV_W - 1, d_ff), F32)],
        compiler_params=pltpu.CompilerParams(dimension_semantics=("arbitrary", "arbitrary"),
                                             vmem_limit_bytes=VMEM_LIMIT_BYTES),
        name="out_ffn",
    )(x2d, y_ret, o_dsa, w_out, g2, w_up, w_gate, w_down, conv_w, conv_b, conv_state, g_final)


def _rope_tables(past, t, rows):
    half = DK_RET // 2
    inv_freq = ROPE_BASE ** (-jnp.arange(half, dtype=F32) / half)
    pos = past + jnp.arange(t, dtype=jnp.int32)
    ang = pos.astype(F32)[:, None] * inv_freq[None, :]
    cos, sin = jnp.cos(ang), jnp.sin(ang)
    cos_t = jnp.concatenate([cos, cos, cos, cos], axis=1)
    sin_t = jnp.concatenate([-sin, sin, -sin, sin], axis=1)
    if t < rows:
        cos_t = jnp.tile(cos_t, (rows // t, 1))
        sin_t = jnp.tile(sin_t, (rows // t, 1))
    return cos_t, sin_t


def _trunk_layer(x, ret_state, past_kv, conv_state, weights, g_final):
    w_in_pad, w_out, g_mix, g_gn, g_ffn, w_up, w_gate, w_down, conv_w, conv_b = weights
    batch, t, d = x.shape
    n = batch * t
    past = 0 if past_kv is None else past_kv[0].shape[1]
    x2d = x.reshape(n, d)

    tm_in = min(512, n)
    cos_t, sin_t = _rope_tables(past, t, tm_in)
    (rq, rk, rv, rg, dq, dk, dv, dkb, dvb, iq, ik, ikb, iw) = _in_proj(x2d, g_mix, w_in_pad, cos_t, sin_t, tm_in)

    y_ret, ret_new = _retention(rq, rk, rv, rg, ret_state, g_gn, batch, t)

    if past_kv is None:
        key_sets = [(t, ikb.reshape(batch, t, D_IDX), dkb.reshape(batch, t, W_DSA), dvb.reshape(batch, t, W_DSA))]
    else:
        past_k, past_v, past_ki = past_kv
        rows = -(-t // LANES) * LANES
        padded = lambda a, w: jnp.pad(a.reshape(batch, t, w), ((0, 0), (0, rows - t), (0, 0)))
        flat = lambda a: a.reshape(batch, past, W_DSA)
        key_sets = [(past, past_ki, flat(past_k), flat(past_v)),
                    (t, padded(ikb, D_IDX), padded(dkb, W_DSA), padded(dvb, W_DSA))]
    tq = min(256, -(-t // LANES) * LANES)
    if t % tq:
        qpad = lambda a: jnp.pad(a.reshape(batch, t, -1), ((0, 0), (0, tq - t), (0, 0))).reshape(batch * tq, -1)
        o_dsa = _dsa(qpad(iq), qpad(dq), qpad(iw), key_sets, batch, tq, past, tq, t)
        o_dsa = o_dsa.reshape(batch, tq, W_DSA)[:, :t].reshape(n, W_DSA)
    else:
        o_dsa = _dsa(iq, dq, iw, key_sets, batch, t, past, tq, tq)

    y, conv_new = _out_ffn(x2d, y_ret, o_dsa, w_out, g_ffn, w_up, w_gate, w_down, conv_w, conv_b,
                           conv_state, g_final, batch, t, min(512, n))
    return (y.reshape(batch, t, d), dk.reshape(batch, t, H_DSA, DH_DSA), dv.reshape(batch, t, H_DSA, DH_DSA),
            ik.reshape(batch, t, D_IDX), ret_new, conv_new)


def kernel(x_prompt, x_sample, cache_dsa_k, cache_dsa_v, cache_idx_k, state_ret, state_ffn_conv, w_in, w_out,
           g_norm_mix, g_gn_ret, g_norm_ffn, w_up, w_gate, w_down, conv_w, conv_b, g_norm_final):
    depth = w_in.shape[0]
    assert depth == 1, "the final norm is fused into the layer's last kernel"
    bp = x_prompt.shape[0]
    d_ff = w_up.shape[-1]
    l = 0
    p_in = w_in.shape[-1]
    pad = (-p_in) % LANES
    weights = (jnp.pad(w_in[l], ((0, 0), (0, pad))).astype(BF16), w_out[l].astype(BF16),
               g_norm_mix[l][None, :], g_gn_ret[l][None, :], g_norm_ffn[l][None, :],
               w_up[l].astype(BF16), w_gate[l].astype(BF16), w_down[l].astype(BF16),
               conv_w[l], conv_b[l][None, :])
    g_final = g_norm_final[None, :]

    yp, pk, pv, pki, pret, pconv = _trunk_layer(
        x_prompt, jnp.zeros((bp, H_RET, DK_RET, DV_RET), F32), None,
        jnp.zeros((bp, CONV_W - 1, d_ff), F32), weights, g_final)
    ys, sk, sv, ski, sret, sconv = _trunk_layer(
        x_sample, state_ret[l], (cache_dsa_k[l], cache_dsa_v[l], cache_idx_k[l]),
        state_ffn_conv[l], weights, g_final)
    st = lambda a: a[None]
    return (yp, ys, st(pk), st(pv), st(pki), st(pret), st(pconv),
            st(sk), st(sv), st(ski), st(sret), st(sconv))
```

```python
import functools

import numpy as np
import jax
import jax.numpy as jnp
from jax import lax
from jax.experimental import pallas as pl
from jax.experimental.pallas import tpu as pltpu

F32 = jnp.float32
BF16 = jnp.bfloat16
I32 = jnp.int32
I16 = jnp.int16
COARSE = jnp.bfloat16

CHUNK = 64
H_RET = 8
DK_RET = 64
DV_RET = 64
H_DSA = 8
DH_DSA = 64
H_IDX = 8
D_IDX = 64
TOPK_MAX = 256
CONV_W = 3
ROPE_BASE = 10000.0
EPS = 1e-6
W_RET = H_RET * DV_RET
W_DSA = H_DSA * DH_DSA

LANES = 128
SUBLANES = 8
HALF_ROWS = 16
DEN_ROWS = 16
KEY_BLOCK = 1024
COUNT_BLOCK = 512
GROUPS = 256
VMEM_LIMIT_BYTES = 56 * 1024 * 1024
INT_MIN = -(2 ** 31)
F32_LOWEST = -3.4028234663852886e38
NEG_BIG = -1e30
LOG2E = 1.4426950408889634
SAFE_DENOM_MIN = 2.0 ** -100
SAFE_DENOM_MAX = 2.0 ** 100


def _dot(a, b):
    return jnp.dot(a, b, preferred_element_type=F32)


def _dot_nt(a, b):
    return lax.dot_general(a, b, (((1,), (1,)), ((), ())), preferred_element_type=F32)


def _dot_tn(a, b):
    return lax.dot_general(a, b, (((0,), (0,)), ((), ())), preferred_element_type=F32)


def _rms(x):
    return x * lax.rsqrt(jnp.mean(x * x, axis=-1, keepdims=True) + EPS)


def _silu(x):
    return x / (1.0 + jnp.exp(-x))


def _inproj_kernel(x_ref, g_ref, w_ref, cos_ref, sin_ref,
                   rq_ref, rk_ref, rv_ref, rg_ref, dq_ref, dk_ref, dv_ref, dkb_ref, dvb_ref,
                   iq_ref, ik_ref, ikb_ref, iw_ref):
    tm = x_ref.shape[0]
    h = (_rms(x_ref[...]) * g_ref[...]).astype(BF16)

    def proj(c0, width):
        return _dot(h, w_ref[:, c0:c0 + width])

    cos = jnp.concatenate([cos_ref[...]] * (W_RET // LANES), axis=1)
    sin = jnp.concatenate([sin_ref[...]] * (W_RET // LANES), axis=1)
    lane = lax.broadcasted_iota(I32, (tm, W_RET), 1)
    first_half = (lane & (DK_RET - 1)) < (DK_RET // 2)

    def rope(p):
        swapped = jnp.where(first_half, pltpu.roll(p, W_RET - DK_RET // 2, 1), pltpu.roll(p, DK_RET // 2, 1))
        return p * cos + swapped * sin

    rq_ref[...] = rope(proj(0, 512)).astype(BF16)
    rk_ref[...] = rope(proj(512, 512)) * (DK_RET ** -0.5)
    rv_ref[...] = proj(1024, 512).astype(BF16)
    rg_ref[...] = proj(1536, 512)
    dq_ref[...] = (proj(2048, 512) * (DH_DSA ** -0.5 * LOG2E)).astype(BF16)
    dk = proj(2560, 512)
    dk_ref[...] = dk
    dkb_ref[...] = dk.astype(BF16)
    dv = proj(3072, 512)
    dv_ref[...] = dv
    dvb_ref[...] = dv.astype(BF16)
    iq_ref[...] = (proj(3584, 512) * (D_IDX ** -0.5)).astype(BF16)
    tail = proj(4096, LANES)
    ik = tail[:, :D_IDX]
    ik_ref[...] = ik
    ikb_ref[...] = ik.astype(BF16)
    iw_ref[...] = tail[:, D_IDX:D_IDX + H_IDX] * (H_IDX ** -0.5)


def _in_proj(x2d, g, w_pad, cos_t, sin_t, tm):
    n, d = x2d.shape
    nt = n // tm
    ntab = cos_t.shape[0] // tm
    row = lambda i: (i, 0)
    const = lambda i: (0, 0)
    tab = lambda i: (i % ntab, 0)
    wide = lambda dt: jax.ShapeDtypeStruct((n, 512), dt)
    out_shape = (wide(BF16), wide(F32), wide(BF16), wide(F32), wide(BF16), wide(F32), wide(F32),
                 wide(BF16), wide(BF16), wide(BF16),
                 jax.ShapeDtypeStruct((n, D_IDX), F32), jax.ShapeDtypeStruct((n, D_IDX), BF16),
                 jax.ShapeDtypeStruct((n, H_IDX), F32))
    out_specs = tuple([pl.BlockSpec((tm, 512), row)] * 10
                      + [pl.BlockSpec((tm, D_IDX), row), pl.BlockSpec((tm, D_IDX), row),
                         pl.BlockSpec((tm, H_IDX), row)])
    return pl.pallas_call(
        _inproj_kernel,
        grid=(nt,),
        in_specs=[pl.BlockSpec((tm, d), row), pl.BlockSpec((1, d), const),
                  pl.BlockSpec(w_pad.shape, const),
                  pl.BlockSpec((tm, LANES), tab), pl.BlockSpec((tm, LANES), tab)],
        out_specs=out_specs,
        out_shape=out_shape,
        compiler_params=pltpu.CompilerParams(dimension_semantics=("arbitrary",),
                                             vmem_limit_bytes=VMEM_LIMIT_BYTES),
        name="in_proj",
    )(x2d, g, w_pad, cos_t, sin_t)


def _ret_kernel(q_ref, k_ref, v_ref, g_ref, s0_ref, dec_ref, xi_ref, zeta_ref, gn_ref,
                y_ref, sout_ref, s_scr, *, gamma_c):
    c = pl.program_id(1)
    chunk = q_ref.shape[0]
    npair = H_RET // 2
    blk_row = lax.broadcasted_iota(I32, (LANES, LANES), 0) < DK_RET
    blk_col = lax.broadcasted_iota(I32, (LANES, LANES), 1) < DV_RET
    own_block = blk_row == blk_col
    low = lax.broadcasted_iota(I32, (chunk, LANES), 1) < DV_RET

    @pl.when(c == 0)
    def _():
        for p in range(npair):
            s_scr[p] = jnp.zeros((LANES, LANES), F32)
            s_scr[p, :DK_RET, :DV_RET] = s0_ref[0, 2 * p]
            s_scr[p, DK_RET:, DV_RET:] = s0_ref[0, 2 * p + 1]

    q = q_ref[...]
    k = k_ref[...]
    v = v_ref[...]
    kb = k.astype(BF16)
    kz = (k * zeta_ref[...]).astype(BF16)
    xi = xi_ref[...]
    normed = []
    for p in range(npair):
        sl = slice(LANES * p, LANES * (p + 1))
        qp, kp, vp, kzp = q[:, sl], kb[:, sl], v[:, sl], kz[:, sl]
        state = s_scr[p]
        q2 = jnp.concatenate([jnp.where(low, qp, jnp.zeros_like(qp)), jnp.where(low, jnp.zeros_like(qp), qp)], axis=0)
        scores = _dot_nt(q2, kp) * dec_ref[p]
        both = _dot(scores.astype(BF16), vp)
        inner = jnp.where(low, both[:chunk], both[chunk:])
        cross = _dot(qp, state.astype(BF16)) * xi[:, sl]
        o = inner + cross
        decay_rows = jnp.where(blk_row, gamma_c[2 * p], gamma_c[2 * p + 1])
        s_scr[p] = decay_rows * state + jnp.where(own_block, _dot_tn(kzp, vp), 0.0)
        inv = 1.0 / DV_RET
        mu = jnp.where(low, jnp.sum(jnp.where(low, o, 0.0), axis=-1, keepdims=True),
                       jnp.sum(jnp.where(low, 0.0, o), axis=-1, keepdims=True)) * inv
        dev = o - mu
        sq = dev * dev
        var = jnp.where(low, jnp.sum(jnp.where(low, sq, 0.0), axis=-1, keepdims=True),
                        jnp.sum(jnp.where(low, 0.0, sq), axis=-1, keepdims=True)) * inv
        normed.append(dev * lax.rsqrt(var + EPS))
    on = jnp.concatenate(normed, axis=1) * gn_ref[...]
    y_ref[...] = (_silu(g_ref[...]) * on).astype(BF16)

    @pl.when(c == pl.num_programs(1) - 1)
    def _():
        for p in range(npair):
            sout_ref[0, 2 * p] = s_scr[p, :DK_RET, :DV_RET]
            sout_ref[0, 2 * p + 1] = s_scr[p, DK_RET:, DV_RET:]


def _retention_constants(c):
    lg = np.log1p(-np.exp2(-5.0 - np.arange(H_RET, dtype=np.float64)))
    idx = np.arange(c, dtype=np.float64)
    diff = idx[:, None] - idx[None, :]
    decay = np.where(diff[None] >= 0, np.exp(lg[:, None, None] * np.maximum(diff, 0.0)[None]), 0.0)
    xi = np.exp(lg[None, :] * (idx + 1.0)[:, None])
    zeta = np.exp(lg[None, :] * (c - 1.0 - idx)[:, None])
    gamma_c = tuple(float(np.float32(g)) for g in np.exp(lg * c))
    rep = lambda a: np.repeat(a, DK_RET, axis=1).astype(np.float32)
    pair_decay = decay.reshape(H_RET // 2, 2 * c, c)
    return pair_decay.astype(np.float32), rep(xi), rep(zeta), gamma_c


def _retention(rq, rk, rv, rg, s0, gn, batch, t):
    c = min(4 * CHUNK, t)
    nc = t // c
    decay, xi, zeta, gamma_c = _retention_constants(c)
    blk = lambda b, j: (b * nc + j, 0)
    const2 = lambda b, j: (0, 0)
    const3 = lambda b, j: (0, 0, 0)
    st = lambda b, j: (b, 0, 0, 0)
    n = batch * t
    return pl.pallas_call(
        functools.partial(_ret_kernel, gamma_c=gamma_c),
        grid=(batch, nc),
        in_specs=[pl.BlockSpec((c, W_RET), blk)] * 4
                 + [pl.BlockSpec((1, H_RET, DK_RET, DV_RET), st),
                    pl.BlockSpec((H_RET // 2, 2 * c, c), const3),
                    pl.BlockSpec((c, W_RET), const2), pl.BlockSpec((c, W_RET), const2),
                    pl.BlockSpec((1, W_RET), const2)],
        out_specs=(pl.BlockSpec((c, W_RET), blk), pl.BlockSpec((1, H_RET, DK_RET, DV_RET), st)),
        out_shape=(jax.ShapeDtypeStruct((n, W_RET), BF16),
                   jax.ShapeDtypeStruct((batch, H_RET, DK_RET, DV_RET), F32)),
        scratch_shapes=[pltpu.VMEM((H_RET // 2, LANES, LANES), F32)],
        compiler_params=pltpu.CompilerParams(dimension_semantics=("arbitrary", "arbitrary"),
                                             vmem_limit_bytes=VMEM_LIMIT_BYTES),
        name="retention",
    )(rq, rk, rv, rg, s0, jnp.asarray(decay), jnp.asarray(xi), jnp.asarray(zeta), gn)


def _log2(n):
    assert n > 0 and n & (n - 1) == 0
    return n.bit_length() - 1


def _dsa_kernel(*refs, segs, tq, tq_live, topk, q_pos0):
    nseg = len(segs)
    iq_ref, dq_ref, iw_ref = refs[:3]
    seg_refs = [refs[3 + 3 * s: 6 + 3 * s] for s in range(nseg)]
    o_ref = refs[3 + 3 * nseg]
    sc_scr, half_scr, gmax_scr, qi_scr, qm_scr, w_scr, m_scr, acc_scr, kmax_scr = refs[4 + 3 * nseg:]

    chunk_shift = _log2(CHUNK)
    qb = pl.program_id(1)
    q_first = q_pos0 + qb * tq
    q_last_chunk = lax.shift_right_logical(q_first + tq - 1, chunk_shift)
    q_chunk = lax.shift_right_logical(q_first + lax.broadcasted_iota(I32, (1, tq), 1), chunk_shift)
    live = lax.broadcasted_iota(I32, (1, tq), 1) < tq_live

    iq_t = iq_ref[...].astype(F32).T
    dq_t = dq_ref[...].astype(F32).T
    w_scr[...] = iw_ref[...].T
    pair_row = lax.broadcasted_iota(I32, (LANES, tq), 0)
    for h in range(H_IDX):
        qi_scr[h] = iq_t[D_IDX * h:D_IDX * (h + 1)].astype(BF16)
    for hp in range(H_DSA // 2):
        pair = dq_t[LANES * hp:LANES * (hp + 1)]
        qm_scr[hp] = jnp.concatenate([jnp.where(pair_row < DH_DSA, pair, 0.0),
                                      jnp.where(pair_row >= DH_DSA, pair, 0.0)], axis=1).astype(BF16)
    acc_scr[...] = jnp.zeros(acc_scr.shape, F32)
    gmax_scr[...] = jnp.full(gmax_scr.shape, -jnp.inf, COARSE)

    def blocks_needed(seg):
        valid, blk, _, pos0 = seg
        adm = jnp.clip((q_last_chunk + 1) * CHUNK - pos0, 0, valid)
        return lax.shift_right_logical(adm + blk - 1, _log2(blk))

    def rows_of(seg, j):
        _, blk, row0, _ = seg
        return pl.ds(pl.multiple_of(row0 + j * blk, blk), blk)

    for seg, (kidx_ref, _, _) in zip(segs, seg_refs):
        valid, blk, _, pos0 = seg

        def score_block(j, carry, seg=seg, kidx_ref=kidx_ref, valid=valid, blk=blk, pos0=pos0):
            kidx = kidx_ref[0, pl.ds(pl.multiple_of(j * blk, blk), blk), :].astype(BF16)
            tot = jnp.zeros((blk, tq), F32)
            for h in range(H_IDX):
                tot = tot + jnp.maximum(_dot(kidx, qi_scr[h]), 0.0) * w_scr[h:h + 1, :]
            idx = j * blk + lax.broadcasted_iota(I32, (blk, tq), 0)
            ok = (lax.shift_right_logical(pos0 + idx, chunk_shift) <= q_chunk) & (idx < valid)
            score = jnp.where(ok, tot, -jnp.inf)
            sc_scr[rows_of(seg, j), :] = score
            near = score.astype(COARSE)
            bits = pltpu.bitcast(near, I16)
            below = pltpu.bitcast(bits + jnp.where(bits < 0, jnp.int16(1), jnp.int16(-1)), COARSE)
            coarse = jnp.where(near.astype(F32) > score, below, near)
            half_scr[rows_of(seg, j), :] = coarse
            grp = min(GROUPS, blk)
            parts = [coarse[grp * r:grp * (r + 1)] for r in range(blk // grp)]
            while len(parts) > 1:
                parts = [jnp.maximum(parts[i], parts[i + 1]) for i in range(0, len(parts), 2)]
            gmax_scr[:grp, :] = jnp.maximum(gmax_scr[:grp, :], parts[0])
            return carry

        lax.fori_loop(0, blocks_needed(seg), score_block, 0)

    def fold_rows(a, rows):
        parts = [a[rows * r:rows * (r + 1)] for r in range(a.shape[0] // rows)]
        while len(parts) > 1:
            parts = [parts[i] + parts[i + 1] for i in range(0, len(parts) - 1, 2)] + parts[len(parts) & ~1:]
        return parts[0]

    def count_rows(scr, rows, one, hit_fn):
        cnt = jnp.zeros((rows, tq), one.dtype)
        for seg in segs:
            seg = (seg[0], min(COUNT_BLOCK, seg[1]), seg[2], seg[3])
            blk, row0 = seg[1], seg[2]

            def body(j, cnt, seg=seg, blk=blk, row0=row0):
                v = scr[rows_of(seg, j), :]
                sub = 16 * rows
                for c in range(0, blk, sub):
                    hit = hit_fn(v[c:c + sub], row0 + j * blk + c)
                    cnt = cnt + fold_rows(jnp.where(hit, one, jnp.zeros_like(one)), rows)
                return cnt

            cnt = lax.fori_loop(0, blocks_needed(seg), body, cnt)
        return jnp.sum(cnt.astype(I32), axis=0, keepdims=True)

    def key_to_f32(key):
        return pltpu.bitcast(jnp.where(key < 0, key ^ jnp.int32(0x7FFFFFFF), key), F32)

    def coarse_bits(key16):
        return jnp.where(key16 < 0, key16 ^ jnp.int32(0x7FFF), key16)

    def largest_passing(n_all, count_ge):
        def step(i, carry):
            ans, n_ans = carry
            cand = jnp.where(i == 0, jnp.zeros_like(ans), ans | jnp.left_shift(jnp.int32(1), 15 - i))
            n = count_ge(cand)
            take = n >= topk
            return jnp.where(take, cand, ans), jnp.where(take, n, n_ans)

        return lax.fori_loop(0, 16, step, (jnp.full((1, tq), -32768, I32), n_all))

    total = jnp.zeros((1, tq), I32)
    for seg in segs:
        total = total + blocks_needed(seg) * seg[1]

    one16 = jnp.ones((1, 1), I16)
    one32 = jnp.ones((1, 1), I32)

    def count_coarse_ge(key16):
        cand = pltpu.bitcast(coarse_bits(key16).astype(I16), COARSE)
        return count_rows(half_scr, HALF_ROWS, one16, lambda v, r0: v >= cand)

    assert topk <= GROUPS
    gm = gmax_scr[...].astype(F32)

    def key16_of(x):
        b = lax.shift_right_arithmetic(pltpu.bitcast(x, I32), 16)
        return jnp.where(b < 0, b ^ jnp.int32(0x7FFF), b)

    lo = key16_of(jnp.min(gm, axis=0, keepdims=True))
    hi = jnp.minimum(key16_of(jnp.max(gm, axis=0, keepdims=True)) + 1, 32767)
    n_lo = count_coarse_ge(lo)
    sound = n_lo >= topk
    lo = jnp.where(sound, lo, -32768)
    n_lo = jnp.where(sound, n_lo, total)
    widest = jnp.max(hi - lo)
    nsteps = sum(jnp.where(widest >= (1 << b), 1, 0) for b in range(17))

    def halve(i, carry):
        lo, hi, n_lo = carry
        mid = lo + lax.shift_right_arithmetic(hi - lo + 1, 1)
        n = count_coarse_ge(mid)
        take = (n >= topk) & (hi > lo)
        return jnp.where(take, mid, lo), jnp.where(take, hi, jnp.minimum(hi, mid - 1)), jnp.where(take, n, n_lo)

    key16, _, n_coarse = lax.fori_loop(0, nsteps, halve, (lo, hi, n_lo))
    short = key16 == -32768
    base = lax.shift_left(coarse_bits(key16), 16)
    base = jnp.where(base < 0, base ^ jnp.int32(0x7FFFFFFF), base)

    def count_fine_ge(off):
        cand = key_to_f32(base + off + 32768)
        return count_rows(sc_scr, SUBLANES, one32, lambda v, r0: v >= cand)

    off, n_ge = largest_passing(n_coarse, count_fine_ge)
    thr = jnp.where(short, -jnp.inf, key_to_f32(jnp.where(short, 0, base + off + 32768)))

    need = live & (n_ge > topk) & jnp.logical_not(short)

    @pl.when(jnp.max(jnp.where(need, 1, 0)) > 0)
    def _():
        def count_scores(hit_fn):
            def with_rows(v, r0):
                return hit_fn(v, r0 + lax.broadcasted_iota(I32, v.shape, 0))

            return count_rows(sc_scr, SUBLANES, one32, with_rows)

        quota = topk - count_scores(lambda v, ridx: v > thr)
        nbits = int(sc_scr.shape[0]).bit_length()

        def index_bisect(i, last):
            cand = last | jnp.left_shift(jnp.int32(1), nbits - 1 - i)
            below = count_scores(lambda v, ridx: (v == thr) & (ridx < cand))
            return jnp.where(below < quota, cand, last)

        last = lax.fori_loop(0, nbits, index_bisect, jnp.zeros((1, tq), I32))

        for seg in segs:
            blk, row0 = seg[1], seg[2]

            def drop(j, carry, seg=seg, blk=blk, row0=row0):
                v = sc_scr[rows_of(seg, j), :]
                ridx = row0 + j * blk + lax.broadcasted_iota(I32, (blk, tq), 0)
                cut = need & (v == thr) & (ridx > last)
                sc_scr[rows_of(seg, j), :] = jnp.where(cut, -jnp.inf, v)
                return carry

            lax.fori_loop(0, blocks_needed(seg), drop, 0)

    thr_sel = jnp.maximum(thr, F32_LOWEST)

    def attend_all(body):
        for seg, (_, k_ref, v_ref) in zip(segs, seg_refs):
            blk = seg[1]

            def step(j, carry, seg=seg, k_ref=k_ref, v_ref=v_ref, blk=blk):
                bias = jnp.where(sc_scr[rows_of(seg, j), :] >= thr_sel, 0.0, NEG_BIG)
                bias = jnp.concatenate([bias, bias], axis=1)
                rows = pl.ds(pl.multiple_of(j * blk, blk), blk)
                kblk = k_ref[0, rows, :].astype(BF16)
                vblk = v_ref[0, rows, :].astype(F32)
                ones = jnp.ones((DEN_ROWS, blk), BF16)
                for hp in range(H_DSA // 2):
                    lanes = slice(LANES * hp, LANES * (hp + 1))
                    s = _dot(kblk[:, lanes], qm_scr[hp]) + bias
                    v_aug = jnp.concatenate([vblk[:, lanes].T.astype(BF16), ones], axis=0)
                    body(hp, s, v_aug)
                return carry

            lax.fori_loop(0, blocks_needed(seg), step, 0)

    @pl.when(qb == 0)
    def _():
        feat = lax.shift_right_logical(lax.broadcasted_iota(I32, (W_DSA, LANES), 0), _log2(DH_DSA))
        head_sum = jnp.where(feat == lax.broadcasted_iota(I32, (W_DSA, LANES), 1), 1.0, 0.0).astype(BF16)
        best = jnp.zeros((1, LANES), F32)
        for seg, (_, k_ref, _) in zip(segs, seg_refs):
            blk = seg[1]

            def norms(j, best, k_ref=k_ref, blk=blk):
                kb = k_ref[0, pl.ds(pl.multiple_of(j * blk, blk), blk), :].astype(F32)
                sq = _dot((kb * kb).astype(BF16), head_sum)
                return jnp.maximum(best, jnp.max(sq, axis=0, keepdims=True))

            best = lax.fori_loop(0, k_ref.shape[1] // blk, norms, best)
        knorm = jnp.sqrt(best)
        for h in range(H_DSA):
            kmax_scr[h // 2, :, tq * (h % 2):tq * (h % 2 + 1)] = jnp.broadcast_to(knorm[:, h:h + 1], (1, tq))

    for hp in range(H_DSA // 2):
        qp = qm_scr[hp].astype(F32)
        m_scr[hp] = jnp.sqrt(jnp.sum(qp * qp, axis=0, keepdims=True)) * kmax_scr[hp]

    def fast_pair(hp, s, v_aug):
        acc_scr[hp] = acc_scr[hp] + _dot(v_aug, jnp.exp2(s - m_scr[hp]).astype(BF16))

    attend_all(fast_pair)

    bad = jnp.zeros((1, 2 * tq), I32)
    for hp in range(H_DSA // 2):
        den = acc_scr[hp][LANES:LANES + 1]
        bad = bad | jnp.where((den >= SAFE_DENOM_MIN) & (den <= SAFE_DENOM_MAX), 0, 1)

    @pl.when(jnp.max(bad) > 0)
    def _():
        m_scr[...] = jnp.full(m_scr.shape, NEG_BIG, F32)
        acc_scr[...] = jnp.zeros(acc_scr.shape, F32)

        def online_pair(hp, s, v_aug):
            m_old = m_scr[hp]
            m_new = jnp.maximum(m_old, jnp.max(s, axis=0, keepdims=True))
            acc_scr[hp] = (jnp.exp2(m_old - m_new) * acc_scr[hp]
                           + _dot(v_aug, jnp.exp2(s - m_new).astype(BF16)))
            m_scr[hp] = m_new

        attend_all(online_pair)

    for hp in range(H_DSA // 2):
        acc = acc_scr[hp]
        den = acc[LANES:LANES + 1]
        pair_t = jnp.concatenate([acc[:DH_DSA, :tq] / den[:, :tq], acc[DH_DSA:LANES, tq:] / den[:, tq:]], axis=0)
        o_ref[:, LANES * hp:LANES * (hp + 1)] = pair_t.T.astype(BF16)


def _dsa(iq, dq, iw, key_sets, batch, t, past, tq, tq_live):
    nq = t // tq
    ltot = sum(ks[0] for ks in key_sets)
    topk = min(TOPK_MAX, ltot // 4)
    segs = []
    row = 0
    pos = 0
    for valid, kidx, _, _ in key_sets:
        length = kidx.shape[1]
        blk = min(KEY_BLOCK, length)
        assert blk % LANES == 0 and length % blk == 0 and valid <= length
        segs.append((valid, blk, row, pos))
        row += length
        pos += valid
    assert topk <= row and tq % LANES == 0
    qrow = lambda b, i: (b * nq + i, 0)
    per_b = lambda b, i: (b, 0, 0)
    in_specs = [pl.BlockSpec((tq, 512), qrow), pl.BlockSpec((tq, 512), qrow), pl.BlockSpec((tq, H_IDX), qrow)]
    operands = [iq, dq, iw]
    for _, kidx, k, v in key_sets:
        length = kidx.shape[1]
        in_specs += [pl.BlockSpec((1, length, D_IDX), per_b), pl.BlockSpec((1, length, W_DSA), per_b),
                     pl.BlockSpec((1, length, W_DSA), per_b)]
        operands += [kidx, k, v]
    return pl.pallas_call(
        functools.partial(_dsa_kernel, segs=tuple(segs), tq=tq, tq_live=tq_live, topk=topk, q_pos0=past),
        grid=(batch, nq),
        in_specs=in_specs,
        out_specs=pl.BlockSpec((tq, W_DSA), qrow),
        out_shape=jax.ShapeDtypeStruct((batch * t, W_DSA), BF16),
        scratch_shapes=[pltpu.VMEM((row, tq), F32),
                        pltpu.VMEM((row, tq), COARSE),
                        pltpu.VMEM((GROUPS, tq), COARSE),
                        pltpu.VMEM((H_IDX, D_IDX, tq), BF16),
                        pltpu.VMEM((H_DSA // 2, LANES, 2 * tq), BF16),
                        pltpu.VMEM((H_IDX, tq), F32),
                        pltpu.VMEM((H_DSA // 2, 1, 2 * tq), F32),
                        pltpu.VMEM((H_DSA // 2, LANES + DEN_ROWS, 2 * tq), F32),
                        pltpu.VMEM((H_DSA // 2, 1, 2 * tq), F32)],
        compiler_params=pltpu.CompilerParams(dimension_semantics=("arbitrary", "arbitrary"),
                                             vmem_limit_bytes=VMEM_LIMIT_BYTES),
        name="dsa",
    )(*operands)


def _ffn_kernel(x_ref, yr_ref, od_ref, wo_ref, g2_ref, wup_ref, wg_ref, wd_ref, cw_ref, cb_ref, cs_ref, gf_ref,
                y_ref, cnew_ref, carry_scr, *, fc):
    t = pl.program_id(1)
    tm = x_ref.shape[0]
    d_ff = wup_ref.shape[1]
    nseq = carry_scr.shape[0]
    seq = tm // nseq

    @pl.when(t == 0)
    def _():
        carry_scr[...] = cs_ref[...]

    x1 = x_ref[...] + _dot(yr_ref[...], wo_ref[:W_RET, :]) + _dot(od_ref[...], wo_ref[W_RET:, :])
    h2 = (_rms(x1) * g2_ref[...]).astype(BF16)
    row = lax.broadcasted_iota(I32, (tm, fc), 0) & (seq - 1)
    acc = jnp.zeros(x1.shape, F32)

    def carried(r, cols):
        if nseq == 1:
            return carry_scr[0, r:r + 1, cols]
        return jnp.concatenate([jnp.broadcast_to(carry_scr[s, r:r + 1, cols], (seq, fc)) for s in range(nseq)], axis=0)

    for c0 in range(0, d_ff, fc):
        cols = slice(c0, c0 + fc)
        a = _dot(h2, wup_ref[:, cols])
        u = _dot(h2, wg_ref[:, cols])
        prev2, prev1 = carried(0, cols), carried(1, cols)
        a1 = jnp.where(row == 0, prev1, pltpu.roll(a, 1, 0))
        a2 = jnp.where(row == 0, prev2, jnp.where(row == 1, prev1, pltpu.roll(a, 2, 0)))
        cw = cw_ref[:, cols]
        conv = cb_ref[:, cols] + a2 * cw[0:1] + a1 * cw[1:2] + a * cw[2:3]
        for s in range(nseq):
            carry_scr[s, :, cols] = a[(s + 1) * seq - 2:(s + 1) * seq, :]
        acc = acc + _dot((_silu(conv) * u).astype(BF16), wd_ref[cols, :])
    y_ref[...] = _rms(x1 + acc) * gf_ref[...]

    @pl.when(t == pl.num_programs(1) - 1)
    def _():
        cnew_ref[...] = carry_scr[...]


def _out_ffn(x2d, y_ret, o_dsa, w_out, g2, w_up, w_gate, w_down, conv_w, conv_b, conv_state, g_final,
             batch, t, tm):
    n, d = x2d.shape
    d_ff = w_up.shape[1]
    nseq = max(1, tm // t)
    assert tm % nseq == 0 and (tm // nseq) & (tm // nseq - 1) == 0 and batch % nseq == 0 and t % (tm // nseq) == 0
    nt = t * nseq // tm
    fc = d_ff
    assert d_ff % fc == 0 and fc % LANES == 0
    row = lambda g, i: (g * nt + i, 0)
    const = lambda g, i: (0, 0)
    per_g = lambda g, i: (g, 0, 0)
    return pl.pallas_call(
        functools.partial(_ffn_kernel, fc=fc),
        grid=(batch // nseq, nt),
        in_specs=[pl.BlockSpec((tm, d), row), pl.BlockSpec((tm, W_RET), row), pl.BlockSpec((tm, W_DSA), row),
                  pl.BlockSpec(w_out.shape, const), pl.BlockSpec((1, d), const),
                  pl.BlockSpec(w_up.shape, const), pl.BlockSpec(w_gate.shape, const),
                  pl.BlockSpec(w_down.shape, const), pl.BlockSpec(conv_w.shape, const),
                  pl.BlockSpec((1, d_ff), const), pl.BlockSpec((nseq, CONV_W - 1, d_ff), per_g),
                  pl.BlockSpec((1, d), const)],
        out_specs=(pl.BlockSpec((tm, d), row), pl.BlockSpec((nseq, CONV_W - 1, d_ff), per_g)),
        out_shape=(jax.ShapeDtypeStruct((n, d), F32), jax.ShapeDtypeStruct((batch, CONV_W - 1, d_ff), F32)),
        scratch_shapes=[pltpu.VMEM((nseq, CONV_W - 1, d_ff), F32)],
        compiler_params=pltpu.CompilerParams(dimension_semantics=("arbitrary", "arbitrary"),
                                             vmem_limit_bytes=VMEM_LIMIT_BYTES),
        name="out_ffn",
    )(x2d, y_ret, o_dsa, w_out, g2, w_up, w_gate, w_down, conv_w, conv_b, conv_state, g_final)


def _rope_tables(past, t, rows):
    half = DK_RET // 2
    inv_freq = ROPE_BASE ** (-jnp.arange(half, dtype=F32) / half)
    pos = past + jnp.arange(t, dtype=jnp.int32)
    ang = pos.astype(F32)[:, None] * inv_freq[None, :]
    cos, sin = jnp.cos(ang), jnp.sin(ang)
    cos_t = jnp.concatenate([cos, cos, cos, cos], axis=1)
    sin_t = jnp.concatenate([-sin, sin, -sin, sin], axis=1)
    if t < rows:
        cos_t = jnp.tile(cos_t, (rows // t, 1))
        sin_t = jnp.tile(sin_t, (rows // t, 1))
    return cos_t, sin_t


def _trunk_layer(x, ret_state, past_kv, conv_state, weights, g_final):
    w_in_pad, w_out, g_mix, g_gn, g_ffn, w_up, w_gate, w_down, conv_w, conv_b = weights
    batch, t, d = x.shape
    n = batch * t
    past = 0 if past_kv is None else past_kv[0].shape[1]
    x2d = x.reshape(n, d)

    tm_in = min(512, n)
    cos_t, sin_t = _rope_tables(past, t, tm_in)
    (rq, rk, rv, rg, dq, dk, dv, dkb, dvb, iq, ik, ikb, iw) = _in_proj(x2d, g_mix, w_in_pad, cos_t, sin_t, tm_in)

    y_ret, ret_new = _retention(rq, rk, rv, rg, ret_state, g_gn, batch, t)

    if past_kv is None:
        key_sets = [(t, ikb.reshape(batch, t, D_IDX), dkb.reshape(batch, t, W_DSA), dvb.reshape(batch, t, W_DSA))]
    else:
        past_k, past_v, past_ki = past_kv
        rows = -(-t // LANES) * LANES
        padded = lambda a, w: jnp.pad(a.reshape(batch, t, w), ((0, 0), (0, rows - t), (0, 0)))
        flat = lambda a: a.reshape(batch, past, W_DSA)
        key_sets = [(past, past_ki, flat(past_k), flat(past_v)),
                    (t, padded(ikb, D_IDX), padded(dkb, W_DSA), padded(dvb, W_DSA))]
    tq = min(256, -(-t // LANES) * LANES)
    if t % tq:
        qpad = lambda a: jnp.pad(a.reshape(batch, t, -1), ((0, 0), (0, tq - t), (0, 0))).reshape(batch * tq, -1)
        o_dsa = _dsa(qpad(iq), qpad(dq), qpad(iw), key_sets, batch, tq, past, tq, t)
        o_dsa = o_dsa.reshape(batch, tq, W_DSA)[:, :t].reshape(n, W_DSA)
    else:
        o_dsa = _dsa(iq, dq, iw, key_sets, batch, t, past, tq, tq)

    y, conv_new = _out_ffn(x2d, y_ret, o_dsa, w_out, g_ffn, w_up, w_gate, w_down, conv_w, conv_b,
                           conv_state, g_final, batch, t, min(512, n))
    return (y.reshape(batch, t, d), dk.reshape(batch, t, H_DSA, DH_DSA), dv.reshape(batch, t, H_DSA, DH_DSA),
            ik.reshape(batch, t, D_IDX), ret_new, conv_new)


def kernel(x_prompt, x_sample, cache_dsa_k, cache_dsa_v, cache_idx_k, state_ret, state_ffn_conv, w_in, w_out,
           g_norm_mix, g_gn_ret, g_norm_ffn, w_up, w_gate, w_down, conv_w, conv_b, g_norm_final):
    depth = w_in.shape[0]
    assert depth == 1, "the final norm is fused into the layer's last kernel"
    bp = x_prompt.shape[0]
    d_ff = w_up.shape[-1]
    l = 0
    p_in = w_in.shape[-1]
    pad = (-p_in) % LANES
    weights = (jnp.pad(w_in[l], ((0, 0), (0, pad))).astype(BF16), w_out[l].astype(BF16),
               g_norm_mix[l][None, :], g_gn_ret[l][None, :], g_norm_ffn[l][None, :],
               w_up[l].astype(BF16), w_gate[l].astype(BF16), w_down[l].astype(BF16),
               conv_w[l], conv_b[l][None, :])
    g_final = g_norm_final[None, :]

    yp, pk, pv, pki, pret, pconv = _trunk_layer(
        x_prompt, jnp.zeros((bp, H_RET, DK_RET, DV_RET), F32), None,
        jnp.zeros((bp, CONV_W - 1, d_ff), F32), weights, g_final)
    ys, sk, sv, ski, sret, sconv = _trunk_layer(
        x_sample, state_ret[l], (cache_dsa_k[l], cache_dsa_v[l], cache_idx_k[l]),
        state_ffn_conv[l], weights, g_final)
    st = lambda a: a[None]
    return (yp, ys, st(pk), st(pv), st(pki), st(pret), st(pconv),
            st(sk), st(sv), st(ski), st(sret), st(sconv))
```

```python
import functools

import numpy as np
import jax
import jax.numpy as jnp
from jax import lax
from jax.experimental import pallas as pl
from jax.experimental.pallas import tpu as pltpu

F32 = jnp.float32
BF16 = jnp.bfloat16
I32 = jnp.int32

CHUNK = 64
H_RET = 8
DK_RET = 64
DV_RET = 64
H_DSA = 8
DH_DSA = 64
H_IDX = 8
D_IDX = 64
TOPK_MAX = 256
CONV_W = 3
ROPE_BASE = 10000.0
EPS = 1e-6
W_RET = H_RET * DV_RET
W_DSA = H_DSA * DH_DSA

LANES = 128
SUBLANES = 8
DEN_ROWS = 16
KEY_BLOCK = 1024
COUNT_BLOCK = 512
VMEM_LIMIT_BYTES = 56 * 1024 * 1024
INT_MIN = -(2 ** 31)
F32_LOWEST = -3.4028234663852886e38
NEG_BIG = -1e30
LOG2E = 1.4426950408889634
SAFE_DENOM_MIN = 2.0 ** -100
SAFE_DENOM_MAX = 2.0 ** 100


def _dot(a, b):
    return jnp.dot(a, b, preferred_element_type=F32)


def _dot_nt(a, b):
    return lax.dot_general(a, b, (((1,), (1,)), ((), ())), preferred_element_type=F32)


def _dot_tn(a, b):
    return lax.dot_general(a, b, (((0,), (0,)), ((), ())), preferred_element_type=F32)


def _rms(x):
    return x * lax.rsqrt(jnp.mean(x * x, axis=-1, keepdims=True) + EPS)


def _silu(x):
    return x / (1.0 + jnp.exp(-x))


def _inproj_kernel(x_ref, g_ref, w_ref, cos_ref, sin_ref,
                   rq_ref, rk_ref, rv_ref, rg_ref, dq_ref, dk_ref, dv_ref, dkb_ref, dvb_ref,
                   iq_ref, ik_ref, ikb_ref, iw_ref):
    tm = x_ref.shape[0]
    h = (_rms(x_ref[...]) * g_ref[...]).astype(BF16)

    def proj(c0, width):
        return _dot(h, w_ref[:, c0:c0 + width])

    cos = jnp.concatenate([cos_ref[...]] * (W_RET // LANES), axis=1)
    sin = jnp.concatenate([sin_ref[...]] * (W_RET // LANES), axis=1)
    lane = lax.broadcasted_iota(I32, (tm, W_RET), 1)
    first_half = (lane & (DK_RET - 1)) < (DK_RET // 2)

    def rope(p):
        swapped = jnp.where(first_half, pltpu.roll(p, W_RET - DK_RET // 2, 1), pltpu.roll(p, DK_RET // 2, 1))
        return p * cos + swapped * sin

    rq_ref[...] = rope(proj(0, 512)).astype(BF16)
    rk_ref[...] = rope(proj(512, 512)) * (DK_RET ** -0.5)
    rv_ref[...] = proj(1024, 512).astype(BF16)
    rg_ref[...] = proj(1536, 512)
    dq_ref[...] = (proj(2048, 512) * (DH_DSA ** -0.5 * LOG2E)).astype(BF16)
    dk = proj(2560, 512)
    dk_ref[...] = dk
    dkb_ref[...] = dk.astype(BF16)
    dv = proj(3072, 512)
    dv_ref[...] = dv
    dvb_ref[...] = dv.astype(BF16)
    iq_ref[...] = (proj(3584, 512) * (D_IDX ** -0.5)).astype(BF16)
    tail = proj(4096, LANES)
    ik = tail[:, :D_IDX]
    ik_ref[...] = ik
    ikb_ref[...] = ik.astype(BF16)
    iw_ref[...] = tail[:, D_IDX:D_IDX + H_IDX] * (H_IDX ** -0.5)


def _in_proj(x2d, g, w_pad, cos_t, sin_t, tm):
    n, d = x2d.shape
    nt = n // tm
    ntab = cos_t.shape[0] // tm
    row = lambda i: (i, 0)
    const = lambda i: (0, 0)
    tab = lambda i: (i % ntab, 0)
    wide = lambda dt: jax.ShapeDtypeStruct((n, 512), dt)
    out_shape = (wide(BF16), wide(F32), wide(BF16), wide(F32), wide(BF16), wide(F32), wide(F32),
                 wide(BF16), wide(BF16), wide(BF16),
                 jax.ShapeDtypeStruct((n, D_IDX), F32), jax.ShapeDtypeStruct((n, D_IDX), BF16),
                 jax.ShapeDtypeStruct((n, H_IDX), F32))
    out_specs = tuple([pl.BlockSpec((tm, 512), row)] * 10
                      + [pl.BlockSpec((tm, D_IDX), row), pl.BlockSpec((tm, D_IDX), row),
                         pl.BlockSpec((tm, H_IDX), row)])
    return pl.pallas_call(
        _inproj_kernel,
        grid=(nt,),
        in_specs=[pl.BlockSpec((tm, d), row), pl.BlockSpec((1, d), const),
                  pl.BlockSpec(w_pad.shape, const),
                  pl.BlockSpec((tm, LANES), tab), pl.BlockSpec((tm, LANES), tab)],
        out_specs=out_specs,
        out_shape=out_shape,
        compiler_params=pltpu.CompilerParams(dimension_semantics=("arbitrary",),
                                             vmem_limit_bytes=VMEM_LIMIT_BYTES),
        name="in_proj",
    )(x2d, g, w_pad, cos_t, sin_t)


def _ret_kernel(q_ref, k_ref, v_ref, g_ref, s0_ref, dec_ref, xi_ref, zeta_ref, gn_ref,
                y_ref, sout_ref, s_scr, *, gamma_c):
    c = pl.program_id(1)
    chunk = q_ref.shape[0]
    npair = H_RET // 2
    blk_row = lax.broadcasted_iota(I32, (LANES, LANES), 0) < DK_RET
    blk_col = lax.broadcasted_iota(I32, (LANES, LANES), 1) < DV_RET
    own_block = blk_row == blk_col
    low = lax.broadcasted_iota(I32, (chunk, LANES), 1) < DV_RET

    @pl.when(c == 0)
    def _():
        for p in range(npair):
            s_scr[p] = jnp.zeros((LANES, LANES), F32)
            s_scr[p, :DK_RET, :DV_RET] = s0_ref[0, 2 * p]
            s_scr[p, DK_RET:, DV_RET:] = s0_ref[0, 2 * p + 1]

    q = q_ref[...]
    k = k_ref[...]
    v = v_ref[...]
    kb = k.astype(BF16)
    kz = (k * zeta_ref[...]).astype(BF16)
    xi = xi_ref[...]
    normed = []
    for p in range(npair):
        sl = slice(LANES * p, LANES * (p + 1))
        qp, kp, vp, kzp = q[:, sl], kb[:, sl], v[:, sl], kz[:, sl]
        state = s_scr[p]
        q2 = jnp.concatenate([jnp.where(low, qp, jnp.zeros_like(qp)), jnp.where(low, jnp.zeros_like(qp), qp)], axis=0)
        scores = _dot_nt(q2, kp) * dec_ref[p]
        both = _dot(scores.astype(BF16), vp)
        inner = jnp.where(low, both[:chunk], both[chunk:])
        cross = _dot(qp, state.astype(BF16)) * xi[:, sl]
        o = inner + cross
        decay_rows = jnp.where(blk_row, gamma_c[2 * p], gamma_c[2 * p + 1])
        s_scr[p] = decay_rows * state + jnp.where(own_block, _dot_tn(kzp, vp), 0.0)
        inv = 1.0 / DV_RET
        mu = jnp.where(low, jnp.sum(jnp.where(low, o, 0.0), axis=-1, keepdims=True),
                       jnp.sum(jnp.where(low, 0.0, o), axis=-1, keepdims=True)) * inv
        dev = o - mu
        sq = dev * dev
        var = jnp.where(low, jnp.sum(jnp.where(low, sq, 0.0), axis=-1, keepdims=True),
                        jnp.sum(jnp.where(low, 0.0, sq), axis=-1, keepdims=True)) * inv
        normed.append(dev * lax.rsqrt(var + EPS))
    on = jnp.concatenate(normed, axis=1) * gn_ref[...]
    y_ref[...] = (_silu(g_ref[...]) * on).astype(BF16)

    @pl.when(c == pl.num_programs(1) - 1)
    def _():
        for p in range(npair):
            sout_ref[0, 2 * p] = s_scr[p, :DK_RET, :DV_RET]
            sout_ref[0, 2 * p + 1] = s_scr[p, DK_RET:, DV_RET:]


def _retention_constants(c):
    lg = np.log1p(-np.exp2(-5.0 - np.arange(H_RET, dtype=np.float64)))
    idx = np.arange(c, dtype=np.float64)
    diff = idx[:, None] - idx[None, :]
    decay = np.where(diff[None] >= 0, np.exp(lg[:, None, None] * np.maximum(diff, 0.0)[None]), 0.0)
    xi = np.exp(lg[None, :] * (idx + 1.0)[:, None])
    zeta = np.exp(lg[None, :] * (c - 1.0 - idx)[:, None])
    gamma_c = tuple(float(np.float32(g)) for g in np.exp(lg * c))
    rep = lambda a: np.repeat(a, DK_RET, axis=1).astype(np.float32)
    pair_decay = decay.reshape(H_RET // 2, 2 * c, c)
    return pair_decay.astype(np.float32), rep(xi), rep(zeta), gamma_c


def _retention(rq, rk, rv, rg, s0, gn, batch, t):
    c = min(4 * CHUNK, t)
    nc = t // c
    decay, xi, zeta, gamma_c = _retention_constants(c)
    blk = lambda b, j: (b * nc + j, 0)
    const2 = lambda b, j: (0, 0)
    const3 = lambda b, j: (0, 0, 0)
    st = lambda b, j: (b, 0, 0, 0)
    n = batch * t
    return pl.pallas_call(
        functools.partial(_ret_kernel, gamma_c=gamma_c),
        grid=(batch, nc),
        in_specs=[pl.BlockSpec((c, W_RET), blk)] * 4
                 + [pl.BlockSpec((1, H_RET, DK_RET, DV_RET), st),
                    pl.BlockSpec((H_RET // 2, 2 * c, c), const3),
                    pl.BlockSpec((c, W_RET), const2), pl.BlockSpec((c, W_RET), const2),
                    pl.BlockSpec((1, W_RET), const2)],
        out_specs=(pl.BlockSpec((c, W_RET), blk), pl.BlockSpec((1, H_RET, DK_RET, DV_RET), st)),
        out_shape=(jax.ShapeDtypeStruct((n, W_RET), BF16),
                   jax.ShapeDtypeStruct((batch, H_RET, DK_RET, DV_RET), F32)),
        scratch_shapes=[pltpu.VMEM((H_RET // 2, LANES, LANES), F32)],
        compiler_params=pltpu.CompilerParams(dimension_semantics=("arbitrary", "arbitrary"),
                                             vmem_limit_bytes=VMEM_LIMIT_BYTES),
        name="retention",
    )(rq, rk, rv, rg, s0, jnp.asarray(decay), jnp.asarray(xi), jnp.asarray(zeta), gn)


def _log2(n):
    assert n > 0 and n & (n - 1) == 0
    return n.bit_length() - 1


def _dsa_kernel(*refs, segs, tq, tq_live, topk, q_pos0):
    nseg = len(segs)
    iq_ref, dq_ref, iw_ref = refs[:3]
    seg_refs = [refs[3 + 3 * s: 6 + 3 * s] for s in range(nseg)]
    o_ref = refs[3 + 3 * nseg]
    sc_scr, qi_scr, qm_scr, w_scr, m_scr, acc_scr, kmax_scr = refs[4 + 3 * nseg:]

    chunk_shift = _log2(CHUNK)
    qb = pl.program_id(1)
    q_first = q_pos0 + qb * tq
    q_last_chunk = lax.shift_right_logical(q_first + tq - 1, chunk_shift)
    q_chunk = lax.shift_right_logical(q_first + lax.broadcasted_iota(I32, (1, tq), 1), chunk_shift)
    live = lax.broadcasted_iota(I32, (1, tq), 1) < tq_live

    iq_t = iq_ref[...].astype(F32).T
    dq_t = dq_ref[...].astype(F32).T
    w_scr[...] = iw_ref[...].T
    pair_row = lax.broadcasted_iota(I32, (LANES, tq), 0)
    for h in range(H_IDX):
        qi_scr[h] = iq_t[D_IDX * h:D_IDX * (h + 1)].astype(BF16)
    for hp in range(H_DSA // 2):
        pair = dq_t[LANES * hp:LANES * (hp + 1)]
        qm_scr[hp] = jnp.concatenate([jnp.where(pair_row < DH_DSA, pair, 0.0),
                                      jnp.where(pair_row >= DH_DSA, pair, 0.0)], axis=1).astype(BF16)
    acc_scr[...] = jnp.zeros(acc_scr.shape, F32)

    def blocks_needed(seg):
        valid, blk, _, pos0 = seg
        adm = jnp.clip((q_last_chunk + 1) * CHUNK - pos0, 0, valid)
        return lax.shift_right_logical(adm + blk - 1, _log2(blk))

    def rows_of(seg, j):
        _, blk, row0, _ = seg
        return pl.ds(pl.multiple_of(row0 + j * blk, blk), blk)

    for seg, (kidx_ref, _, _) in zip(segs, seg_refs):
        valid, blk, _, pos0 = seg

        def score_block(j, carry, seg=seg, kidx_ref=kidx_ref, valid=valid, blk=blk, pos0=pos0):
            kidx = kidx_ref[0, pl.ds(pl.multiple_of(j * blk, blk), blk), :].astype(BF16)
            tot = jnp.zeros((blk, tq), F32)
            for h in range(H_IDX):
                tot = tot + jnp.maximum(_dot(kidx, qi_scr[h]), 0.0) * w_scr[h:h + 1, :]
            idx = j * blk + lax.broadcasted_iota(I32, (blk, tq), 0)
            ok = (lax.shift_right_logical(pos0 + idx, chunk_shift) <= q_chunk) & (idx < valid)
            score = jnp.where(ok, tot, -jnp.inf)
            sc_scr[rows_of(seg, j), :] = score
            return carry

        lax.fori_loop(0, blocks_needed(seg), score_block, 0)

    def fold_rows(a, rows):
        parts = [a[rows * r:rows * (r + 1)] for r in range(a.shape[0] // rows)]
        while len(parts) > 1:
            parts = [parts[i] + parts[i + 1] for i in range(0, len(parts) - 1, 2)] + parts[len(parts) & ~1:]
        return parts[0]

    def count_rows(scr, rows, one, hit_fn):
        cnt = jnp.zeros((rows, tq), one.dtype)
        for seg in segs:
            seg = (seg[0], min(COUNT_BLOCK, seg[1]), seg[2], seg[3])
            blk, row0 = seg[1], seg[2]

            def body(j, cnt, seg=seg, blk=blk, row0=row0):
                v = scr[rows_of(seg, j), :]
                sub = 16 * rows
                for c in range(0, blk, sub):
                    hit = hit_fn(v[c:c + sub], row0 + j * blk + c)
                    cnt = cnt + fold_rows(jnp.where(hit, one, jnp.zeros_like(one)), rows)
                return cnt

            cnt = lax.fori_loop(0, blocks_needed(seg), body, cnt)
        return jnp.sum(cnt.astype(I32), axis=0, keepdims=True)

    def key_to_f32(key):
        return pltpu.bitcast(jnp.where(key < 0, key ^ jnp.int32(0x7FFFFFFF), key), F32)

    total = jnp.zeros((1, tq), I32)
    for seg in segs:
        total = total + blocks_needed(seg) * seg[1]

    one32 = jnp.ones((1, 1), I32)

    def step(i, carry):
        ans, n_ans = carry
        key = jnp.where(i == 0, jnp.zeros_like(ans), ans | jnp.left_shift(jnp.int32(1), 31 - i))
        cand = key_to_f32(key)
        n = count_rows(sc_scr, SUBLANES, one32, lambda v, r0: v >= cand)
        take = n >= topk
        return jnp.where(take, key, ans), jnp.where(take, n, n_ans)

    thr_key, n_ge = lax.fori_loop(0, 32, step, (jnp.full((1, tq), INT_MIN, I32), total))
    short = thr_key == INT_MIN
    thr = jnp.where(short, -jnp.inf, key_to_f32(jnp.where(short, 0, thr_key)))

    need = live & (n_ge > topk) & jnp.logical_not(short)

    @pl.when(jnp.max(jnp.where(need, 1, 0)) > 0)
    def _():
        def count_scores(hit_fn):
            def with_rows(v, r0):
                return hit_fn(v, r0 + lax.broadcasted_iota(I32, v.shape, 0))

            return count_rows(sc_scr, SUBLANES, one32, with_rows)

        quota = topk - count_scores(lambda v, ridx: v > thr)
        nbits = int(sc_scr.shape[0]).bit_length()

        def index_bisect(i, last):
            cand = last | jnp.left_shift(jnp.int32(1), nbits - 1 - i)
            below = count_scores(lambda v, ridx: (v == thr) & (ridx < cand))
            return jnp.where(below < quota, cand, last)

        last = lax.fori_loop(0, nbits, index_bisect, jnp.zeros((1, tq), I32))

        for seg in segs:
            blk, row0 = seg[1], seg[2]

            def drop(j, carry, seg=seg, blk=blk, row0=row0):
                v = sc_scr[rows_of(seg, j), :]
                ridx = row0 + j * blk + lax.broadcasted_iota(I32, (blk, tq), 0)
                cut = need & (v == thr) & (ridx > last)
                sc_scr[rows_of(seg, j), :] = jnp.where(cut, -jnp.inf, v)
                return carry

            lax.fori_loop(0, blocks_needed(seg), drop, 0)

    thr_sel = jnp.maximum(thr, F32_LOWEST)

    def attend_all(body):
        for seg, (_, k_ref, v_ref) in zip(segs, seg_refs):
            blk = seg[1]

            def step(j, carry, seg=seg, k_ref=k_ref, v_ref=v_ref, blk=blk):
                bias = jnp.where(sc_scr[rows_of(seg, j), :] >= thr_sel, 0.0, NEG_BIG)
                bias = jnp.concatenate([bias, bias], axis=1)
                rows = pl.ds(pl.multiple_of(j * blk, blk), blk)
                kblk = k_ref[0, rows, :].astype(BF16)
                vblk = v_ref[0, rows, :].astype(F32)
                ones = jnp.ones((DEN_ROWS, blk), BF16)
                for hp in range(H_DSA // 2):
                    lanes = slice(LANES * hp, LANES * (hp + 1))
                    s = _dot(kblk[:, lanes], qm_scr[hp]) + bias
                    v_aug = jnp.concatenate([vblk[:, lanes].T.astype(BF16), ones], axis=0)
                    body(hp, s, v_aug)
                return carry

            lax.fori_loop(0, blocks_needed(seg), step, 0)

    @pl.when(qb == 0)
    def _():
        feat = lax.shift_right_logical(lax.broadcasted_iota(I32, (W_DSA, LANES), 0), _log2(DH_DSA))
        head_sum = jnp.where(feat == lax.broadcasted_iota(I32, (W_DSA, LANES), 1), 1.0, 0.0).astype(BF16)
        best = jnp.zeros((1, LANES), F32)
        for seg, (_, k_ref, _) in zip(segs, seg_refs):
            blk = seg[1]

            def norms(j, best, k_ref=k_ref, blk=blk):
                kb = k_ref[0, pl.ds(pl.multiple_of(j * blk, blk), blk), :].astype(F32)
                sq = _dot((kb * kb).astype(BF16), head_sum)
                return jnp.maximum(best, jnp.max(sq, axis=0, keepdims=True))

            best = lax.fori_loop(0, k_ref.shape[1] // blk, norms, best)
        knorm = jnp.sqrt(best)
        for h in range(H_DSA):
            kmax_scr[h // 2, :, tq * (h % 2):tq * (h % 2 + 1)] = jnp.broadcast_to(knorm[:, h:h + 1], (1, tq))

    for hp in range(H_DSA // 2):
        qp = qm_scr[hp].astype(F32)
        m_scr[hp] = jnp.sqrt(jnp.sum(qp * qp, axis=0, keepdims=True)) * kmax_scr[hp]

    def fast_pair(hp, s, v_aug):
        acc_scr[hp] = acc_scr[hp] + _dot(v_aug, jnp.exp2(s - m_scr[hp]).astype(BF16))

    attend_all(fast_pair)

    bad = jnp.zeros((1, 2 * tq), I32)
    for hp in range(H_DSA // 2):
        den = acc_scr[hp][LANES:LANES + 1]
        bad = bad | jnp.where((den >= SAFE_DENOM_MIN) & (den <= SAFE_DENOM_MAX), 0, 1)

    @pl.when(jnp.max(bad) > 0)
    def _():
        m_scr[...] = jnp.full(m_scr.shape, NEG_BIG, F32)
        acc_scr[...] = jnp.zeros(acc_scr.shape, F32)

        def online_pair(hp, s, v_aug):
            m_old = m_scr[hp]
            m_new = jnp.maximum(m_old, jnp.max(s, axis=0, keepdims=True))
            acc_scr[hp] = (jnp.exp2(m_old - m_new) * acc_scr[hp]
                           + _dot(v_aug, jnp.exp2(s - m_new).astype(BF16)))
            m_scr[hp] = m_new

        attend_all(online_pair)

    for hp in range(H_DSA // 2):
        acc = acc_scr[hp]
        den = acc[LANES:LANES + 1]
        pair_t = jnp.concatenate([acc[:DH_DSA, :tq] / den[:, :tq], acc[DH_DSA:LANES, tq:] / den[:, tq:]], axis=0)
        o_ref[:, LANES * hp:LANES * (hp + 1)] = pair_t.T.astype(BF16)


def _dsa(iq, dq, iw, key_sets, batch, t, past, tq, tq_live):
    nq = t // tq
    ltot = sum(ks[0] for ks in key_sets)
    topk = min(TOPK_MAX, ltot // 4)
    segs = []
    row = 0
    pos = 0
    for valid, kidx, _, _ in key_sets:
        length = kidx.shape[1]
        blk = min(KEY_BLOCK, length)
        assert blk % LANES == 0 and length % blk == 0 and valid <= length
        segs.append((valid, blk, row, pos))
        row += length
        pos += valid
    assert topk <= row and tq % LANES == 0
    qrow = lambda b, i: (b * nq + i, 0)
    per_b = lambda b, i: (b, 0, 0)
    in_specs = [pl.BlockSpec((tq, 512), qrow), pl.BlockSpec((tq, 512), qrow), pl.BlockSpec((tq, H_IDX), qrow)]
    operands = [iq, dq, iw]
    for _, kidx, k, v in key_sets:
        length = kidx.shape[1]
        in_specs += [pl.BlockSpec((1, length, D_IDX), per_b), pl.BlockSpec((1, length, W_DSA), per_b),
                     pl.BlockSpec((1, length, W_DSA), per_b)]
        operands += [kidx, k, v]
    return pl.pallas_call(
        functools.partial(_dsa_kernel, segs=tuple(segs), tq=tq, tq_live=tq_live, topk=topk, q_pos0=past),
        grid=(batch, nq),
        in_specs=in_specs,
        out_specs=pl.BlockSpec((tq, W_DSA), qrow),
        out_shape=jax.ShapeDtypeStruct((batch * t, W_DSA), BF16),
        scratch_shapes=[pltpu.VMEM((row, tq), F32),
                        pltpu.VMEM((H_IDX, D_IDX, tq), BF16),
                        pltpu.VMEM((H_DSA // 2, LANES, 2 * tq), BF16),
                        pltpu.VMEM((H_IDX, tq), F32),
                        pltpu.VMEM((H_DSA // 2, 1, 2 * tq), F32),
                        pltpu.VMEM((H_DSA // 2, LANES + DEN_ROWS, 2 * tq), F32),
                        pltpu.VMEM((H_DSA // 2, 1, 2 * tq), F32)],
        compiler_params=pltpu.CompilerParams(dimension_semantics=("arbitrary", "arbitrary"),
                                             vmem_limit_bytes=VMEM_LIMIT_BYTES),
        name="dsa",
    )(*operands)


def _ffn_kernel(x_ref, yr_ref, od_ref, wo_ref, g2_ref, wup_ref, wg_ref, wd_ref, cw_ref, cb_ref, cs_ref, gf_ref,
                y_ref, cnew_ref, carry_scr, *, fc):
    t = pl.program_id(1)
    tm = x_ref.shape[0]
    d_ff = wup_ref.shape[1]
    nseq = carry_scr.shape[0]
    seq = tm // nseq

    @pl.when(t == 0)
    def _():
        carry_scr[...] = cs_ref[...]

    x1 = x_ref[...] + _dot(yr_ref[...], wo_ref[:W_RET, :]) + _dot(od_ref[...], wo_ref[W_RET:, :])
    h2 = (_rms(x1) * g2_ref[...]).astype(BF16)
    row = lax.broadcasted_iota(I32, (tm, fc), 0) & (seq - 1)
    acc = jnp.zeros(x1.shape, F32)

    def carried(r, cols):
        if nseq == 1:
            return carry_scr[0, r:r + 1, cols]
        return jnp.concatenate([jnp.broadcast_to(carry_scr[s, r:r + 1, cols], (seq, fc)) for s in range(nseq)], axis=0)

    for c0 in range(0, d_ff, fc):
        cols = slice(c0, c0 + fc)
        a = _dot(h2, wup_ref[:, cols])
        u = _dot(h2, wg_ref[:, cols])
        prev2, prev1 = carried(0, cols), carried(1, cols)
        a1 = jnp.where(row == 0, prev1, pltpu.roll(a, 1, 0))
        a2 = jnp.where(row == 0, prev2, jnp.where(row == 1, prev1, pltpu.roll(a, 2, 0)))
        cw = cw_ref[:, cols]
        conv = cb_ref[:, cols] + a2 * cw[0:1] + a1 * cw[1:2] + a * cw[2:3]
        for s in range(nseq):
            carry_scr[s, :, cols] = a[(s + 1) * seq - 2:(s + 1) * seq, :]
        acc = acc + _dot((_silu(conv) * u).astype(BF16), wd_ref[cols, :])
    y_ref[...] = _rms(x1 + acc) * gf_ref[...]

    @pl.when(t == pl.num_programs(1) - 1)
    def _():
        cnew_ref[...] = carry_scr[...]


def _out_ffn(x2d, y_ret, o_dsa, w_out, g2, w_up, w_gate, w_down, conv_w, conv_b, conv_state, g_final,
             batch, t, tm):
    n, d = x2d.shape
    d_ff = w_up.shape[1]
    nseq = max(1, tm // t)
    assert tm % nseq == 0 and (tm // nseq) & (tm // nseq - 1) == 0 and batch % nseq == 0 and t % (tm // nseq) == 0
    nt = t * nseq // tm
    fc = d_ff
    assert d_ff % fc == 0 and fc % LANES == 0
    row = lambda g, i: (g * nt + i, 0)
    const = lambda g, i: (0, 0)
    per_g = lambda g, i: (g, 0, 0)
    return pl.pallas_call(
        functools.partial(_ffn_kernel, fc=fc),
        grid=(batch // nseq, nt),
        in_specs=[pl.BlockSpec((tm, d), row), pl.BlockSpec((tm, W_RET), row), pl.BlockSpec((tm, W_DSA), row),
                  pl.BlockSpec(w_out.shape, const), pl.BlockSpec((1, d), const),
                  pl.BlockSpec(w_up.shape, const), pl.BlockSpec(w_gate.shape, const),
                  pl.BlockSpec(w_down.shape, const), pl.BlockSpec(conv_w.shape, const),
                  pl.BlockSpec((1, d_ff), const), pl.BlockSpec((nseq, CONV_W - 1, d_ff), per_g),
                  pl.BlockSpec((1, d), const)],
        out_specs=(pl.BlockSpec((tm, d), row), pl.BlockSpec((nseq, CONV_W - 1, d_ff), per_g)),
        out_shape=(jax.ShapeDtypeStruct((n, d), F32), jax.ShapeDtypeStruct((batch, CONV_W - 1, d_ff), F32)),
        scratch_shapes=[pltpu.VMEM((nseq, CONV_W - 1, d_ff), F32)],
        compiler_params=pltpu.CompilerParams(dimension_semantics=("arbitrary", "arbitrary"),
                                             vmem_limit_bytes=VMEM_LIMIT_BYTES),
        name="out_ffn",
    )(x2d, y_ret, o_dsa, w_out, g2, w_up, w_gate, w_down, conv_w, conv_b, conv_state, g_final)


def _rope_tables(past, t, rows):
    half = DK_RET // 2
    inv_freq = ROPE_BASE ** (-jnp.arange(half, dtype=F32) / half)
    pos = past + jnp.arange(t, dtype=jnp.int32)
    ang = pos.astype(F32)[:, None] * inv_freq[None, :]
    cos, sin = jnp.cos(ang), jnp.sin(ang)
    cos_t = jnp.concatenate([cos, cos, cos, cos], axis=1)
    sin_t = jnp.concatenate([-sin, sin, -sin, sin], axis=1)
    if t < rows:
        cos_t = jnp.tile(cos_t, (rows // t, 1))
        sin_t = jnp.tile(sin_t, (rows // t, 1))
    return cos_t, sin_t


def _trunk_layer(x, ret_state, past_kv, conv_state, weights, g_final):
    w_in_pad, w_out, g_mix, g_gn, g_ffn, w_up, w_gate, w_down, conv_w, conv_b = weights
    batch, t, d = x.shape
    n = batch * t
    past = 0 if past_kv is None else past_kv[0].shape[1]
    x2d = x.reshape(n, d)

    tm_in = min(512, n)
    cos_t, sin_t = _rope_tables(past, t, tm_in)
    (rq, rk, rv, rg, dq, dk, dv, dkb, dvb, iq, ik, ikb, iw) = _in_proj(x2d, g_mix, w_in_pad, cos_t, sin_t, tm_in)

    y_ret, ret_new = _retention(rq, rk, rv, rg, ret_state, g_gn, batch, t)

    if past_kv is None:
        key_sets = [(t, ikb.reshape(batch, t, D_IDX), dkb.reshape(batch, t, W_DSA), dvb.reshape(batch, t, W_DSA))]
    else:
        past_k, past_v, past_ki = past_kv
        rows = -(-t // LANES) * LANES
        padded = lambda a, w: jnp.pad(a.reshape(batch, t, w), ((0, 0), (0, rows - t), (0, 0)))
        flat = lambda a: a.reshape(batch, past, W_DSA)
        key_sets = [(past, past_ki, flat(past_k), flat(past_v)),
                    (t, padded(ikb, D_IDX), padded(dkb, W_DSA), padded(dvb, W_DSA))]
    tq = min(256, -(-t // LANES) * LANES)
    if t % tq:
        qpad = lambda a: jnp.pad(a.reshape(batch, t, -1), ((0, 0), (0, tq - t), (0, 0))).reshape(batch * tq, -1)
        o_dsa = _dsa(qpad(iq), qpad(dq), qpad(iw), key_sets, batch, tq, past, tq, t)
        o_dsa = o_dsa.reshape(batch, tq, W_DSA)[:, :t].reshape(n, W_DSA)
    else:
        o_dsa = _dsa(iq, dq, iw, key_sets, batch, t, past, tq, tq)

    y, conv_new = _out_ffn(x2d, y_ret, o_dsa, w_out, g_ffn, w_up, w_gate, w_down, conv_w, conv_b,
                           conv_state, g_final, batch, t, min(512, n))
    return (y.reshape(batch, t, d), dk.reshape(batch, t, H_DSA, DH_DSA), dv.reshape(batch, t, H_DSA, DH_DSA),
            ik.reshape(batch, t, D_IDX), ret_new, conv_new)


def kernel(x_prompt, x_sample, cache_dsa_k, cache_dsa_v, cache_idx_k, state_ret, state_ffn_conv, w_in, w_out,
           g_norm_mix, g_gn_ret, g_norm_ffn, w_up, w_gate, w_down, conv_w, conv_b, g_norm_final):
    depth = w_in.shape[0]
    assert depth == 1, "the final norm is fused into the layer's last kernel"
    bp = x_prompt.shape[0]
    d_ff = w_up.shape[-1]
    l = 0
    p_in = w_in.shape[-1]
    pad = (-p_in) % LANES
    weights = (jnp.pad(w_in[l], ((0, 0), (0, pad))).astype(BF16), w_out[l].astype(BF16),
               g_norm_mix[l][None, :], g_gn_ret[l][None, :], g_norm_ffn[l][None, :],
               w_up[l].astype(BF16), w_gate[l].astype(BF16), w_down[l].astype(BF16),
               conv_w[l], conv_b[l][None, :])
    g_final = g_norm_final[None, :]

    yp, pk, pv, pki, pret, pconv = _trunk_layer(
        x_prompt, jnp.zeros((bp, H_RET, DK_RET, DV_RET), F32), None,
        jnp.zeros((bp, CONV_W - 1, d_ff), F32), weights, g_final)
    ys, sk, sv, ski, sret, sconv = _trunk_layer(
        x_sample, state_ret[l], (cache_dsa_k[l], cache_dsa_v[l], cache_idx_k[l]),
        state_ffn_conv[l], weights, g_final)
    st = lambda a: a[None]
    return (yp, ys, st(pk), st(pv), st(pki), st(pret), st(pconv),
            st(sk), st(sv), st(ski), st(sret), st(sconv))
```

```python
import functools

import numpy as np
import jax
import jax.numpy as jnp
from jax import lax
from jax.experimental import pallas as pl
from jax.experimental.pallas import tpu as pltpu

F32 = jnp.float32
BF16 = jnp.bfloat16
I32 = jnp.int32
I16 = jnp.int16
COARSE = jnp.bfloat16

CHUNK = 64
H_RET = 8
DK_RET = 64
DV_RET = 64
H_DSA = 8
DH_DSA = 64
H_IDX = 8
D_IDX = 64
TOPK_MAX = 256
CONV_W = 3
ROPE_BASE = 10000.0
EPS = 1e-6
W_RET = H_RET * DV_RET
W_DSA = H_DSA * DH_DSA

LANES = 128
SUBLANES = 8
HALF_ROWS = 16
DEN_ROWS = 16
KEY_BLOCK = 1024
COUNT_BLOCK = 512
VMEM_LIMIT_BYTES = 56 * 1024 * 1024
INT_MIN = -(2 ** 31)
F32_LOWEST = -3.4028234663852886e38
NEG_BIG = -1e30
LOG2E = 1.4426950408889634
SAFE_DENOM_MIN = 2.0 ** -100
SAFE_DENOM_MAX = 2.0 ** 100


def _dot(a, b):
    return jnp.dot(a, b, preferred_element_type=F32)


def _dot_nt(a, b):
    return lax.dot_general(a, b, (((1,), (1,)), ((), ())), preferred_element_type=F32)


def _dot_tn(a, b):
    return lax.dot_general(a, b, (((0,), (0,)), ((), ())), preferred_element_type=F32)


def _rms(x):
    return x * lax.rsqrt(jnp.mean(x * x, axis=-1, keepdims=True) + EPS)


def _silu(x):
    return x / (1.0 + jnp.exp(-x))


def _inproj_kernel(x_ref, g_ref, w_ref, cos_ref, sin_ref,
                   rq_ref, rk_ref, rv_ref, rg_ref, dq_ref, dk_ref, dv_ref, dkb_ref, dvb_ref,
                   iq_ref, ik_ref, ikb_ref, iw_ref):
    tm = x_ref.shape[0]
    h = (_rms(x_ref[...]) * g_ref[...]).astype(BF16)

    def proj(c0, width):
        return _dot(h, w_ref[:, c0:c0 + width])

    cos = jnp.concatenate([cos_ref[...]] * (W_RET // LANES), axis=1)
    sin = jnp.concatenate([sin_ref[...]] * (W_RET // LANES), axis=1)
    lane = lax.broadcasted_iota(I32, (tm, W_RET), 1)
    first_half = (lane & (DK_RET - 1)) < (DK_RET // 2)

    def rope(p):
        swapped = jnp.where(first_half, pltpu.roll(p, W_RET - DK_RET // 2, 1), pltpu.roll(p, DK_RET // 2, 1))
        return p * cos + swapped * sin

    rq_ref[...] = rope(proj(0, 512)).astype(BF16)
    rk_ref[...] = rope(proj(512, 512)) * (DK_RET ** -0.5)
    rv_ref[...] = proj(1024, 512).astype(BF16)
    rg_ref[...] = proj(1536, 512)
    dq_ref[...] = (proj(2048, 512) * (DH_DSA ** -0.5 * LOG2E)).astype(BF16)
    dk = proj(2560, 512)
    dk_ref[...] = dk
    dkb_ref[...] = dk.astype(BF16)
    dv = proj(3072, 512)
    dv_ref[...] = dv
    dvb_ref[...] = dv.astype(BF16)
    iq_ref[...] = (proj(3584, 512) * (D_IDX ** -0.5)).astype(BF16)
    tail = proj(4096, LANES)
    ik = tail[:, :D_IDX]
    ik_ref[...] = ik
    ikb_ref[...] = ik.astype(BF16)
    iw_ref[...] = tail[:, D_IDX:D_IDX + H_IDX] * (H_IDX ** -0.5)


def _in_proj(x2d, g, w_pad, cos_t, sin_t, tm):
    n, d = x2d.shape
    nt = n // tm
    ntab = cos_t.shape[0] // tm
    row = lambda i: (i, 0)
    const = lambda i: (0, 0)
    tab = lambda i: (i % ntab, 0)
    wide = lambda dt: jax.ShapeDtypeStruct((n, 512), dt)
    out_shape = (wide(BF16), wide(F32), wide(BF16), wide(F32), wide(BF16), wide(F32), wide(F32),
                 wide(BF16), wide(BF16), wide(BF16),
                 jax.ShapeDtypeStruct((n, D_IDX), F32), jax.ShapeDtypeStruct((n, D_IDX), BF16),
                 jax.ShapeDtypeStruct((n, H_IDX), F32))
    out_specs = tuple([pl.BlockSpec((tm, 512), row)] * 10
                      + [pl.BlockSpec((tm, D_IDX), row), pl.BlockSpec((tm, D_IDX), row),
                         pl.BlockSpec((tm, H_IDX), row)])
    return pl.pallas_call(
        _inproj_kernel,
        grid=(nt,),
        in_specs=[pl.BlockSpec((tm, d), row), pl.BlockSpec((1, d), const),
                  pl.BlockSpec(w_pad.shape, const),
                  pl.BlockSpec((tm, LANES), tab), pl.BlockSpec((tm, LANES), tab)],
        out_specs=out_specs,
        out_shape=out_shape,
        compiler_params=pltpu.CompilerParams(dimension_semantics=("arbitrary",),
                                             vmem_limit_bytes=VMEM_LIMIT_BYTES),
        name="in_proj",
    )(x2d, g, w_pad, cos_t, sin_t)


def _ret_kernel(q_ref, k_ref, v_ref, g_ref, s0_ref, dec_ref, xi_ref, zeta_ref, gn_ref,
                y_ref, sout_ref, s_scr, *, gamma_c):
    c = pl.program_id(1)
    chunk = q_ref.shape[0]
    npair = H_RET // 2
    blk_row = lax.broadcasted_iota(I32, (LANES, LANES), 0) < DK_RET
    blk_col = lax.broadcasted_iota(I32, (LANES, LANES), 1) < DV_RET
    own_block = blk_row == blk_col
    low = lax.broadcasted_iota(I32, (chunk, LANES), 1) < DV_RET

    @pl.when(c == 0)
    def _():
        for p in range(npair):
            s_scr[p] = jnp.zeros((LANES, LANES), F32)
            s_scr[p, :DK_RET, :DV_RET] = s0_ref[0, 2 * p]
            s_scr[p, DK_RET:, DV_RET:] = s0_ref[0, 2 * p + 1]

    q = q_ref[...]
    k = k_ref[...]
    v = v_ref[...]
    kb = k.astype(BF16)
    kz = (k * zeta_ref[...]).astype(BF16)
    xi = xi_ref[...]
    normed = []
    for p in range(npair):
        sl = slice(LANES * p, LANES * (p + 1))
        qp, kp, vp, kzp = q[:, sl], kb[:, sl], v[:, sl], kz[:, sl]
        state = s_scr[p]
        q2 = jnp.concatenate([jnp.where(low, qp, jnp.zeros_like(qp)), jnp.where(low, jnp.zeros_like(qp), qp)], axis=0)
        scores = _dot_nt(q2, kp) * dec_ref[p]
        both = _dot(scores.astype(BF16), vp)
        inner = jnp.where(low, both[:chunk], both[chunk:])
        cross = _dot(qp, state.astype(BF16)) * xi[:, sl]
        o = inner + cross
        decay_rows = jnp.where(blk_row, gamma_c[2 * p], gamma_c[2 * p + 1])
        s_scr[p] = decay_rows * state + jnp.where(own_block, _dot_tn(kzp, vp), 0.0)
        inv = 1.0 / DV_RET
        mu = jnp.where(low, jnp.sum(jnp.where(low, o, 0.0), axis=-1, keepdims=True),
                       jnp.sum(jnp.where(low, 0.0, o), axis=-1, keepdims=True)) * inv
        dev = o - mu
        sq = dev * dev
        var = jnp.where(low, jnp.sum(jnp.where(low, sq, 0.0), axis=-1, keepdims=True),
                        jnp.sum(jnp.where(low, 0.0, sq), axis=-1, keepdims=True)) * inv
        normed.append(dev * lax.rsqrt(var + EPS))
    on = jnp.concatenate(normed, axis=1) * gn_ref[...]
    y_ref[...] = (_silu(g_ref[...]) * on).astype(BF16)

    @pl.when(c == pl.num_programs(1) - 1)
    def _():
        for p in range(npair):
            sout_ref[0, 2 * p] = s_scr[p, :DK_RET, :DV_RET]
            sout_ref[0, 2 * p + 1] = s_scr[p, DK_RET:, DV_RET:]


def _retention_constants(c):
    lg = np.log1p(-np.exp2(-5.0 - np.arange(H_RET, dtype=np.float64)))
    idx = np.arange(c, dtype=np.float64)
    diff = idx[:, None] - idx[None, :]
    decay = np.where(diff[None] >= 0, np.exp(lg[:, None, None] * np.maximum(diff, 0.0)[None]), 0.0)
    xi = np.exp(lg[None, :] * (idx + 1.0)[:, None])
    zeta = np.exp(lg[None, :] * (c - 1.0 - idx)[:, None])
    gamma_c = tuple(float(np.float32(g)) for g in np.exp(lg * c))
    rep = lambda a: np.repeat(a, DK_RET, axis=1).astype(np.float32)
    pair_decay = decay.reshape(H_RET // 2, 2 * c, c)
    return pair_decay.astype(np.float32), rep(xi), rep(zeta), gamma_c


def _retention(rq, rk, rv, rg, s0, gn, batch, t):
    c = min(4 * CHUNK, t)
    nc = t // c
    decay, xi, zeta, gamma_c = _retention_constants(c)
    blk = lambda b, j: (b * nc + j, 0)
    const2 = lambda b, j: (0, 0)
    const3 = lambda b, j: (0, 0, 0)
    st = lambda b, j: (b, 0, 0, 0)
    n = batch * t
    return pl.pallas_call(
        functools.partial(_ret_kernel, gamma_c=gamma_c),
        grid=(batch, nc),
        in_specs=[pl.BlockSpec((c, W_RET), blk)] * 4
                 + [pl.BlockSpec((1, H_RET, DK_RET, DV_RET), st),
                    pl.BlockSpec((H_RET // 2, 2 * c, c), const3),
                    pl.BlockSpec((c, W_RET), const2), pl.BlockSpec((c, W_RET), const2),
                    pl.BlockSpec((1, W_RET), const2)],
        out_specs=(pl.BlockSpec((c, W_RET), blk), pl.BlockSpec((1, H_RET, DK_RET, DV_RET), st)),
        out_shape=(jax.ShapeDtypeStruct((n, W_RET), BF16),
                   jax.ShapeDtypeStruct((batch, H_RET, DK_RET, DV_RET), F32)),
        scratch_shapes=[pltpu.VMEM((H_RET // 2, LANES, LANES), F32)],
        compiler_params=pltpu.CompilerParams(dimension_semantics=("arbitrary", "arbitrary"),
                                             vmem_limit_bytes=VMEM_LIMIT_BYTES),
        name="retention",
    )(rq, rk, rv, rg, s0, jnp.asarray(decay), jnp.asarray(xi), jnp.asarray(zeta), gn)


def _log2(n):
    assert n > 0 and n & (n - 1) == 0
    return n.bit_length() - 1


def _dsa_kernel(*refs, segs, tq, tq_live, topk, q_pos0):
    nseg = len(segs)
    iq_ref, dq_ref, iw_ref = refs[:3]
    seg_refs = [refs[3 + 3 * s: 6 + 3 * s] for s in range(nseg)]
    o_ref = refs[3 + 3 * nseg]
    sc_scr, half_scr, qi_scr, qm_scr, w_scr, m_scr, acc_scr, kmax_scr = refs[4 + 3 * nseg:]

    chunk_shift = _log2(CHUNK)
    qb = pl.program_id(1)
    q_first = q_pos0 + qb * tq
    q_last_chunk = lax.shift_right_logical(q_first + tq - 1, chunk_shift)
    q_chunk = lax.shift_right_logical(q_first + lax.broadcasted_iota(I32, (1, tq), 1), chunk_shift)
    live = lax.broadcasted_iota(I32, (1, tq), 1) < tq_live

    iq_t = iq_ref[...].astype(F32).T
    dq_t = dq_ref[...].astype(F32).T
    w_scr[...] = iw_ref[...].T
    pair_row = lax.broadcasted_iota(I32, (LANES, tq), 0)
    for h in range(H_IDX):
        qi_scr[h] = iq_t[D_IDX * h:D_IDX * (h + 1)].astype(BF16)
    for hp in range(H_DSA // 2):
        pair = dq_t[LANES * hp:LANES * (hp + 1)]
        qm_scr[hp] = jnp.concatenate([jnp.where(pair_row < DH_DSA, pair, 0.0),
                                      jnp.where(pair_row >= DH_DSA, pair, 0.0)], axis=1).astype(BF16)
    acc_scr[...] = jnp.zeros(acc_scr.shape, F32)

    def blocks_needed(seg):
        valid, blk, _, pos0 = seg
        adm = jnp.clip((q_last_chunk + 1) * CHUNK - pos0, 0, valid)
        return lax.shift_right_logical(adm + blk - 1, _log2(blk))

    def rows_of(seg, j):
        _, blk, row0, _ = seg
        return pl.ds(pl.multiple_of(row0 + j * blk, blk), blk)

    for seg, (kidx_ref, _, _) in zip(segs, seg_refs):
        valid, blk, _, pos0 = seg

        def score_block(j, carry, seg=seg, kidx_ref=kidx_ref, valid=valid, blk=blk, pos0=pos0):
            kidx = kidx_ref[0, pl.ds(pl.multiple_of(j * blk, blk), blk), :].astype(BF16)
            tot = jnp.zeros((blk, tq), F32)
            for h in range(H_IDX):
                tot = tot + jnp.maximum(_dot(kidx, qi_scr[h]), 0.0) * w_scr[h:h + 1, :]
            idx = j * blk + lax.broadcasted_iota(I32, (blk, tq), 0)
            ok = (lax.shift_right_logical(pos0 + idx, chunk_shift) <= q_chunk) & (idx < valid)
            score = jnp.where(ok, tot, -jnp.inf)
            sc_scr[rows_of(seg, j), :] = score
            near = score.astype(COARSE)
            bits = pltpu.bitcast(near, I16)
            below = pltpu.bitcast(bits + jnp.where(bits < 0, jnp.int16(1), jnp.int16(-1)), COARSE)
            half_scr[rows_of(seg, j), :] = jnp.where(near.astype(F32) > score, below, near)
            return carry

        lax.fori_loop(0, blocks_needed(seg), score_block, 0)

    def fold_rows(a, rows):
        parts = [a[rows * r:rows * (r + 1)] for r in range(a.shape[0] // rows)]
        while len(parts) > 1:
            parts = [parts[i] + parts[i + 1] for i in range(0, len(parts) - 1, 2)] + parts[len(parts) & ~1:]
        return parts[0]

    def count_rows(scr, rows, one, hit_fn):
        cnt = jnp.zeros((rows, tq), one.dtype)
        for seg in segs:
            seg = (seg[0], min(COUNT_BLOCK, seg[1]), seg[2], seg[3])
            blk, row0 = seg[1], seg[2]

            def body(j, cnt, seg=seg, blk=blk, row0=row0):
                v = scr[rows_of(seg, j), :]
                sub = 16 * rows
                for c in range(0, blk, sub):
                    hit = hit_fn(v[c:c + sub], row0 + j * blk + c)
                    cnt = cnt + fold_rows(jnp.where(hit, one, jnp.zeros_like(one)), rows)
                return cnt

            cnt = lax.fori_loop(0, blocks_needed(seg), body, cnt)
        return jnp.sum(cnt.astype(I32), axis=0, keepdims=True)

    def key_to_f32(key):
        return pltpu.bitcast(jnp.where(key < 0, key ^ jnp.int32(0x7FFFFFFF), key), F32)

    def coarse_bits(key16):
        return jnp.where(key16 < 0, key16 ^ jnp.int32(0x7FFF), key16)

    def largest_passing(n_all, count_ge):
        def step(i, carry):
            ans, n_ans = carry
            cand = jnp.where(i == 0, jnp.zeros_like(ans), ans | jnp.left_shift(jnp.int32(1), 15 - i))
            n = count_ge(cand)
            take = n >= topk
            return jnp.where(take, cand, ans), jnp.where(take, n, n_ans)

        return lax.fori_loop(0, 16, step, (jnp.full((1, tq), -32768, I32), n_all))

    total = jnp.zeros((1, tq), I32)
    for seg in segs:
        total = total + blocks_needed(seg) * seg[1]

    one16 = jnp.ones((1, 1), I16)
    one32 = jnp.ones((1, 1), I32)

    def count_coarse_ge(key16):
        cand = pltpu.bitcast(coarse_bits(key16).astype(I16), COARSE)
        return count_rows(half_scr, HALF_ROWS, one16, lambda v, r0: v >= cand)

    key16, n_coarse = largest_passing(total, count_coarse_ge)
    short = key16 == -32768
    base = lax.shift_left(coarse_bits(key16), 16)
    base = jnp.where(base < 0, base ^ jnp.int32(0x7FFFFFFF), base)

    def count_fine_ge(off):
        cand = key_to_f32(base + off + 32768)
        return count_rows(sc_scr, SUBLANES, one32, lambda v, r0: v >= cand)

    off, n_ge = largest_passing(n_coarse, count_fine_ge)
    thr = jnp.where(short, -jnp.inf, key_to_f32(jnp.where(short, 0, base + off + 32768)))

    need = live & (n_ge > topk) & jnp.logical_not(short)

    @pl.when(jnp.max(jnp.where(need, 1, 0)) > 0)
    def _():
        def count_scores(hit_fn):
            def with_rows(v, r0):
                return hit_fn(v, r0 + lax.broadcasted_iota(I32, v.shape, 0))

            return count_rows(sc_scr, SUBLANES, one32, with_rows)

        quota = topk - count_scores(lambda v, ridx: v > thr)
        nbits = int(sc_scr.shape[0]).bit_length()

        def index_bisect(i, last):
            cand = last | jnp.left_shift(jnp.int32(1), nbits - 1 - i)
            below = count_scores(lambda v, ridx: (v == thr) & (ridx < cand))
            return jnp.where(below < quota, cand, last)

        last = lax.fori_loop(0, nbits, index_bisect, jnp.zeros((1, tq), I32))

        for seg in segs:
            blk, row0 = seg[1], seg[2]

            def drop(j, carry, seg=seg, blk=blk, row0=row0):
                v = sc_scr[rows_of(seg, j), :]
                ridx = row0 + j * blk + lax.broadcasted_iota(I32, (blk, tq), 0)
                cut = need & (v == thr) & (ridx > last)
                sc_scr[rows_of(seg, j), :] = jnp.where(cut, -jnp.inf, v)
                return carry

            lax.fori_loop(0, blocks_needed(seg), drop, 0)

    thr_sel = jnp.maximum(thr, F32_LOWEST)

    def attend_all(body):
        for seg, (_, k_ref, v_ref) in zip(segs, seg_refs):
            blk = seg[1]

            def step(j, carry, seg=seg, k_ref=k_ref, v_ref=v_ref, blk=blk):
                bias = jnp.where(sc_scr[rows_of(seg, j), :] >= thr_sel, 0.0, NEG_BIG)
                bias = jnp.concatenate([bias, bias], axis=1)
                rows = pl.ds(pl.multiple_of(j * blk, blk), blk)
                kblk = k_ref[0, rows, :].astype(BF16)
                vblk = v_ref[0, rows, :].astype(F32)
                ones = jnp.ones((DEN_ROWS, blk), BF16)
                for hp in range(H_DSA // 2):
                    lanes = slice(LANES * hp, LANES * (hp + 1))
                    s = _dot(kblk[:, lanes], qm_scr[hp]) + bias
                    v_aug = jnp.concatenate([vblk[:, lanes].T.astype(BF16), ones], axis=0)
                    body(hp, s, v_aug)
                return carry

            lax.fori_loop(0, blocks_needed(seg), step, 0)

    @pl.when(qb == 0)
    def _():
        feat = lax.shift_right_logical(lax.broadcasted_iota(I32, (W_DSA, LANES), 0), _log2(DH_DSA))
        head_sum = jnp.where(feat == lax.broadcasted_iota(I32, (W_DSA, LANES), 1), 1.0, 0.0).astype(BF16)
        best = jnp.zeros((1, LANES), F32)
        for seg, (_, k_ref, _) in zip(segs, seg_refs):
            blk = seg[1]

            def norms(j, best, k_ref=k_ref, blk=blk):
                kb = k_ref[0, pl.ds(pl.multiple_of(j * blk, blk), blk), :].astype(F32)
                sq = _dot((kb * kb).astype(BF16), head_sum)
                return jnp.maximum(best, jnp.max(sq, axis=0, keepdims=True))

            best = lax.fori_loop(0, k_ref.shape[1] // blk, norms, best)
        knorm = jnp.sqrt(best)
        for h in range(H_DSA):
            kmax_scr[h // 2, :, tq * (h % 2):tq * (h % 2 + 1)] = jnp.broadcast_to(knorm[:, h:h + 1], (1, tq))

    for hp in range(H_DSA // 2):
        qp = qm_scr[hp].astype(F32)
        m_scr[hp] = jnp.sqrt(jnp.sum(qp * qp, axis=0, keepdims=True)) * kmax_scr[hp]

    def fast_pair(hp, s, v_aug):
        acc_scr[hp] = acc_scr[hp] + _dot(v_aug, jnp.exp2(s - m_scr[hp]).astype(BF16))

    attend_all(fast_pair)

    bad = jnp.zeros((1, 2 * tq), I32)
    for hp in range(H_DSA // 2):
        den = acc_scr[hp][LANES:LANES + 1]
        bad = bad | jnp.where((den >= SAFE_DENOM_MIN) & (den <= SAFE_DENOM_MAX), 0, 1)

    @pl.when(jnp.max(bad) > 0)
    def _():
        m_scr[...] = jnp.full(m_scr.shape, NEG_BIG, F32)
        acc_scr[...] = jnp.zeros(acc_scr.shape, F32)

        def online_pair(hp, s, v_aug):
            m_old = m_scr[hp]
            m_new = jnp.maximum(m_old, jnp.max(s, axis=0, keepdims=True))
            acc_scr[hp] = (jnp.exp2(m_old - m_new) * acc_scr[hp]
                           + _dot(v_aug, jnp.exp2(s - m_new).astype(BF16)))
            m_scr[hp] = m_new

        attend_all(online_pair)

    for hp in range(H_DSA // 2):
        acc = acc_scr[hp]
        den = acc[LANES:LANES + 1]
        pair_t = jnp.concatenate([acc[:DH_DSA, :tq] / den[:, :tq], acc[DH_DSA:LANES, tq:] / den[:, tq:]], axis=0)
        o_ref[:, LANES * hp:LANES * (hp + 1)] = pair_t.T.astype(BF16)


def _dsa(iq, dq, iw, key_sets, batch, t, past, tq, tq_live):
    nq = t // tq
    ltot = sum(ks[0] for ks in key_sets)
    topk = min(TOPK_MAX, ltot // 4)
    segs = []
    row = 0
    pos = 0
    for valid, kidx, _, _ in key_sets:
        length = kidx.shape[1]
        blk = min(KEY_BLOCK, length)
        assert blk % LANES == 0 and length % blk == 0 and valid <= length
        segs.append((valid, blk, row, pos))
        row += length
        pos += valid
    assert topk <= row and tq % LANES == 0
    qrow = lambda b, i: (b * nq + i, 0)
    per_b = lambda b, i: (b, 0, 0)
    in_specs = [pl.BlockSpec((tq, 512), qrow), pl.BlockSpec((tq, 512), qrow), pl.BlockSpec((tq, H_IDX), qrow)]
    operands = [iq, dq, iw]
    for _, kidx, k, v in key_sets:
        length = kidx.shape[1]
        mode = dict(pipeline_mode=pl.Buffered(1)) if nq > 1 else {}
        in_specs += [pl.BlockSpec((1, length, D_IDX), per_b, **mode), pl.BlockSpec((1, length, W_DSA), per_b, **mode),
                     pl.BlockSpec((1, length, W_DSA), per_b, **mode)]
        operands += [kidx, k, v]
    return pl.pallas_call(
        functools.partial(_dsa_kernel, segs=tuple(segs), tq=tq, tq_live=tq_live, topk=topk, q_pos0=past),
        grid=(batch, nq),
        in_specs=in_specs,
        out_specs=pl.BlockSpec((tq, W_DSA), qrow),
        out_shape=jax.ShapeDtypeStruct((batch * t, W_DSA), BF16),
        scratch_shapes=[pltpu.VMEM((row, tq), F32),
                        pltpu.VMEM((row, tq), COARSE),
                        pltpu.VMEM((H_IDX, D_IDX, tq), BF16),
                        pltpu.VMEM((H_DSA // 2, LANES, 2 * tq), BF16),
                        pltpu.VMEM((H_IDX, tq), F32),
                        pltpu.VMEM((H_DSA // 2, 1, 2 * tq), F32),
                        pltpu.VMEM((H_DSA // 2, LANES + DEN_ROWS, 2 * tq), F32),
                        pltpu.VMEM((H_DSA // 2, 1, 2 * tq), F32)],
        compiler_params=pltpu.CompilerParams(dimension_semantics=("arbitrary", "arbitrary"),
                                             vmem_limit_bytes=VMEM_LIMIT_BYTES),
        name="dsa",
    )(*operands)


def _ffn_kernel(x_ref, yr_ref, od_ref, wo_ref, g2_ref, wup_ref, wg_ref, wd_ref, cw_ref, cb_ref, cs_ref, gf_ref,
                y_ref, cnew_ref, carry_scr, *, fc):
    t = pl.program_id(1)
    tm = x_ref.shape[0]
    d_ff = wup_ref.shape[1]
    nseq = carry_scr.shape[0]
    seq = tm // nseq

    @pl.when(t == 0)
    def _():
        carry_scr[...] = cs_ref[...]

    x1 = x_ref[...] + _dot(yr_ref[...], wo_ref[:W_RET, :]) + _dot(od_ref[...], wo_ref[W_RET:, :])
    h2 = (_rms(x1) * g2_ref[...]).astype(BF16)
    row = lax.broadcasted_iota(I32, (tm, fc), 0) & (seq - 1)
    acc = jnp.zeros(x1.shape, F32)

    def carried(r, cols):
        if nseq == 1:
            return carry_scr[0, r:r + 1, cols]
        return jnp.concatenate([jnp.broadcast_to(carry_scr[s, r:r + 1, cols], (seq, fc)) for s in range(nseq)], axis=0)

    for c0 in range(0, d_ff, fc):
        cols = slice(c0, c0 + fc)
        a = _dot(h2, wup_ref[:, cols])
        u = _dot(h2, wg_ref[:, cols])
        prev2, prev1 = carried(0, cols), carried(1, cols)
        a1 = jnp.where(row == 0, prev1, pltpu.roll(a, 1, 0))
        a2 = jnp.where(row == 0, prev2, jnp.where(row == 1, prev1, pltpu.roll(a, 2, 0)))
        cw = cw_ref[:, cols]
        conv = cb_ref[:, cols] + a2 * cw[0:1] + a1 * cw[1:2] + a * cw[2:3]
        for s in range(nseq):
            carry_scr[s, :, cols] = a[(s + 1) * seq - 2:(s + 1) * seq, :]
        acc = acc + _dot((_silu(conv) * u).astype(BF16), wd_ref[cols, :])
    y_ref[...] = _rms(x1 + acc) * gf_ref[...]

    @pl.when(t == pl.num_programs(1) - 1)
    def _():
        cnew_ref[...] = carry_scr[...]


def _out_ffn(x2d, y_ret, o_dsa, w_out, g2, w_up, w_gate, w_down, conv_w, conv_b, conv_state, g_final,
             batch, t, tm):
    n, d = x2d.shape
    d_ff = w_up.shape[1]
    nseq = max(1, tm // t)
    assert tm % nseq == 0 and (tm // nseq) & (tm // nseq - 1) == 0 and batch % nseq == 0 and t % (tm // nseq) == 0
    nt = t * nseq // tm
    fc = d_ff
    assert d_ff % fc == 0 and fc % LANES == 0
    row = lambda g, i: (g * nt + i, 0)
    const = lambda g, i: (0, 0)
    per_g = lambda g, i: (g, 0, 0)
    return pl.pallas_call(
        functools.partial(_ffn_kernel, fc=fc),
        grid=(batch // nseq, nt),
        in_specs=[pl.BlockSpec((tm, d), row), pl.BlockSpec((tm, W_RET), row), pl.BlockSpec((tm, W_DSA), row),
                  pl.BlockSpec(w_out.shape, const), pl.BlockSpec((1, d), const),
                  pl.BlockSpec(w_up.shape, const), pl.BlockSpec(w_gate.shape, const),
                  pl.BlockSpec(w_down.shape, const), pl.BlockSpec(conv_w.shape, const),
                  pl.BlockSpec((1, d_ff), const), pl.BlockSpec((nseq, CONV_W - 1, d_ff), per_g),
                  pl.BlockSpec((1, d), const)],
        out_specs=(pl.BlockSpec((tm, d), row), pl.BlockSpec((nseq, CONV_W - 1, d_ff), per_g)),
        out_shape=(jax.ShapeDtypeStruct((n, d), F32), jax.ShapeDtypeStruct((batch, CONV_W - 1, d_ff), F32)),
        scratch_shapes=[pltpu.VMEM((nseq, CONV_W - 1, d_ff), F32)],
        compiler_params=pltpu.CompilerParams(dimension_semantics=("arbitrary", "arbitrary"),
                                             vmem_limit_bytes=VMEM_LIMIT_BYTES),
        name="out_ffn",
    )(x2d, y_ret, o_dsa, w_out, g2, w_up, w_gate, w_down, conv_w, conv_b, conv_state, g_final)


def _rope_tables(past, t, rows):
    half = DK_RET // 2
    inv_freq = ROPE_BASE ** (-jnp.arange(half, dtype=F32) / half)
    pos = past + jnp.arange(t, dtype=jnp.int32)
    ang = pos.astype(F32)[:, None] * inv_freq[None, :]
    cos, sin = jnp.cos(ang), jnp.sin(ang)
    cos_t = jnp.concatenate([cos, cos, cos, cos], axis=1)
    sin_t = jnp.concatenate([-sin, sin, -sin, sin], axis=1)
    if t < rows:
        cos_t = jnp.tile(cos_t, (rows // t, 1))
        sin_t = jnp.tile(sin_t, (rows // t, 1))
    return cos_t, sin_t


def _trunk_layer(x, ret_state, past_kv, conv_state, weights, g_final):
    w_in_pad, w_out, g_mix, g_gn, g_ffn, w_up, w_gate, w_down, conv_w, conv_b = weights
    batch, t, d = x.shape
    n = batch * t
    past = 0 if past_kv is None else past_kv[0].shape[1]
    x2d = x.reshape(n, d)

    tm_in = min(512, n)
    cos_t, sin_t = _rope_tables(past, t, tm_in)
    (rq, rk, rv, rg, dq, dk, dv, dkb, dvb, iq, ik, ikb, iw) = _in_proj(x2d, g_mix, w_in_pad, cos_t, sin_t, tm_in)

    y_ret, ret_new = _retention(rq, rk, rv, rg, ret_state, g_gn, batch, t)

    if past_kv is None:
        key_sets = [(t, ikb.reshape(batch, t, D_IDX), dkb.reshape(batch, t, W_DSA), dvb.reshape(batch, t, W_DSA))]
    else:
        past_k, past_v, past_ki = past_kv
        rows = -(-t // LANES) * LANES
        padded = lambda a, w: jnp.pad(a.reshape(batch, t, w), ((0, 0), (0, rows - t), (0, 0)))
        flat = lambda a: a.reshape(batch, past, W_DSA)
        key_sets = [(past, past_ki, flat(past_k), flat(past_v)),
                    (t, padded(ikb, D_IDX), padded(dkb, W_DSA), padded(dvb, W_DSA))]
    tq = min(512, -(-t // LANES) * LANES)
    if t % tq:
        qpad = lambda a: jnp.pad(a.reshape(batch, t, -1), ((0, 0), (0, tq - t), (0, 0))).reshape(batch * tq, -1)
        o_dsa = _dsa(qpad(iq), qpad(dq), qpad(iw), key_sets, batch, tq, past, tq, t)
        o_dsa = o_dsa.reshape(batch, tq, W_DSA)[:, :t].reshape(n, W_DSA)
    else:
        o_dsa = _dsa(iq, dq, iw, key_sets, batch, t, past, tq, tq)

    y, conv_new = _out_ffn(x2d, y_ret, o_dsa, w_out, g_ffn, w_up, w_gate, w_down, conv_w, conv_b,
                           conv_state, g_final, batch, t, min(512, n))
    return (y.reshape(batch, t, d), dk.reshape(batch, t, H_DSA, DH_DSA), dv.reshape(batch, t, H_DSA, DH_DSA),
            ik.reshape(batch, t, D_IDX), ret_new, conv_new)


def kernel(x_prompt, x_sample, cache_dsa_k, cache_dsa_v, cache_idx_k, state_ret, state_ffn_conv, w_in, w_out,
           g_norm_mix, g_gn_ret, g_norm_ffn, w_up, w_gate, w_down, conv_w, conv_b, g_norm_final):
    depth = w_in.shape[0]
    assert depth == 1, "the final norm is fused into the layer's last kernel"
    bp = x_prompt.shape[0]
    d_ff = w_up.shape[-1]
    l = 0
    p_in = w_in.shape[-1]
    pad = (-p_in) % LANES
    weights = (jnp.pad(w_in[l], ((0, 0), (0, pad))).astype(BF16), w_out[l].astype(BF16),
               g_norm_mix[l][None, :], g_gn_ret[l][None, :], g_norm_ffn[l][None, :],
               w_up[l].astype(BF16), w_gate[l].astype(BF16), w_down[l].astype(BF16),
               conv_w[l], conv_b[l][None, :])
    g_final = g_norm_final[None, :]

    yp, pk, pv, pki, pret, pconv = _trunk_layer(
        x_prompt, jnp.zeros((bp, H_RET, DK_RET, DV_RET), F32), None,
        jnp.zeros((bp, CONV_W - 1, d_ff), F32), weights, g_final)
    ys, sk, sv, ski, sret, sconv = _trunk_layer(
        x_sample, state_ret[l], (cache_dsa_k[l], cache_dsa_v[l], cache_idx_k[l]),
        state_ffn_conv[l], weights, g_final)
    st = lambda a: a[None]
    return (yp, ys, st(pk), st(pv), st(pki), st(pret), st(pconv),
            st(sk), st(sv), st(ski), st(sret), st(sconv))
```
